```python
import math
import jax, jax.numpy as jnp
from jax import lax
import numpy as np

D_MODEL = 1024
BATCH = 16
SEQ = 2048
DEPTH = 1

HEAD_DIM = 64
ATTN_HEADS = 8
ATTN_KV_HEADS = 2
ATTN_WIDTH = ATTN_HEADS * HEAD_DIM
KV_WIDTH = ATTN_KV_HEADS * HEAD_DIM
WINDOW = 128
BLOCK = 128
RWKV_HEADS = 8
RWKV_WIDTH = RWKV_HEADS * HEAD_DIM
MIX_WIDTH = ATTN_WIDTH + RWKV_WIDTH
DECAY_LORA = 64
ICLR_LORA = 64
GATE_LORA = 128
ATTN_COLS = ATTN_WIDTH + 2 * KV_WIDTH
RWKV_COLS = 3 * RWKV_WIDTH + DECAY_LORA + ICLR_LORA + GATE_LORA
IN_COLS = ATTN_COLS + RWKV_COLS
PEER_HEADS = 8
PEER_NKEYS = 128
PEER_EXPERTS = PEER_NKEYS * PEER_NKEYS
PEER_QDIM = 256
PEER_HALF = PEER_QDIM // 2
PEER_TOPK = 16
PEER_CHUNK = 128
LN_EPS = 1e-5
GN_EPS = 64e-5

kernel_name = 'hymba_swa_sink_rwkv7_peer_deepnorm_adaln'


def _layer_norm(x, g, b):
    xf = x.astype(jnp.float32)
    mu = jnp.mean(xf, axis=-1, keepdims=True)
    var = jnp.mean(jnp.square(xf - mu), axis=-1, keepdims=True)
    return ((xf - mu) * lax.rsqrt(var + LN_EPS)).astype(x.dtype) * g + b


def _sliding_window_sink_attention(q, k, v, sink):
    B, S, H, Dh = q.shape
    KV = k.shape[2]
    G = H // KV
    NB = S // BLOCK
    qb = q.reshape(B, NB, BLOCK, KV, G, Dh)
    kb = k.reshape(B, NB, BLOCK, KV, Dh)
    vb = v.reshape(B, NB, BLOCK, KV, Dh)
    prev = lambda t: jnp.concatenate([jnp.zeros_like(t[:, :1]), t[:, :-1]], axis=1)
    kw = jnp.concatenate([prev(kb), kb], axis=2)
    vw = jnp.concatenate([prev(vb), vb], axis=2)
    s = jnp.einsum('bnqkgd,bnjkd->bnkgqj', qb, kw).astype(jnp.float32) * (Dh ** -0.5)
    qpos = jnp.arange(BLOCK)[:, None] + BLOCK
    kpos = jnp.arange(2 * BLOCK)[None, :]
    dist = qpos - kpos
    allowed = (dist >= 0) & (dist < WINDOW)
    valid = (jnp.arange(NB)[:, None] * BLOCK - BLOCK + kpos) >= 0
    mask = allowed[None, :, :] & valid[:, None, :]
    s = jnp.where(mask[None, :, None, None, :, :], s, -jnp.inf)
    sink_l = sink.astype(jnp.float32).reshape(KV, G)[None, None, :, :, None, None]
    m = jnp.maximum(jnp.max(s, axis=-1, keepdims=True), sink_l)
    p = jnp.exp(s - m)
    p = p / (jnp.sum(p, axis=-1, keepdims=True) + jnp.exp(sink_l - m))
    o = jnp.einsum('bnkgqj,bnjkd->bnqkgd', p.astype(v.dtype), vw)
    return o.reshape(B, S, H * Dh)


def _rwkv7_time_mix(p, mu, w0, w2, a0, a2, g2, k_k, k_a, r_k, lnx_g, lnx_b):
    B, S, _ = p.shape
    p_prev = jnp.pad(p, ((0, 0), (1, 0), (0, 0)))[:, :-1]
    p = p + (p_prev - p) * mu
    o0 = RWKV_WIDTH
    r = p[..., 0:o0]
    k = p[..., o0:2 * o0]
    v = p[..., 2 * o0:3 * o0]
    wd = p[..., 3 * o0:3 * o0 + DECAY_LORA]
    ad = p[..., 3 * o0 + DECAY_LORA:3 * o0 + DECAY_LORA + ICLR_LORA]
    gd = p[..., 3 * o0 + DECAY_LORA + ICLR_LORA:]
    w_log = -jax.nn.softplus(-(w0 + jnp.tanh(wd) @ w2)) - 0.5
    decay = jnp.exp(-jnp.exp(w_log.astype(jnp.float32)))
    a = jax.nn.sigmoid(a0 + ad @ a2)
    g = jax.nn.sigmoid(gd) @ g2
    hs = lambda t: t.reshape(B, S, RWKV_HEADS, HEAD_DIM)
    r, k, v, a, decay = hs(r), hs(k), hs(v), hs(a), hs(decay)
    kk = hs(k.reshape(B, S, RWKV_WIDTH) * k_k).astype(jnp.float32)
    kk = kk * lax.rsqrt(jnp.maximum(jnp.sum(kk * kk, axis=-1, keepdims=True), 1e-24))
    k = k * (1.0 + (a - 1.0) * hs(jnp.broadcast_to(k_a, (B, S, RWKV_WIDTH))))
    f32 = jnp.float32
    tm = lambda t: jnp.moveaxis(t.astype(f32), 1, 0)
    xs = (tm(r), tm(decay), tm(k), tm(v), tm(-kk), tm(kk * a.astype(f32)))

    def step(state, inp):
        r_t, w_t, k_t, v_t, na_t, b_t = inp
        sa = jnp.einsum('bhvk,bhk->bhv', state, na_t)
        state = state * w_t[:, :, None, :] + sa[..., None] * b_t[:, :, None, :] + v_t[..., None] * k_t[:, :, None, :]
        return state, jnp.einsum('bhvk,bhk->bhv', state, r_t)

    state0 = jnp.zeros((B, RWKV_HEADS, HEAD_DIM, HEAD_DIM), f32)
    _, ys = lax.scan(step, state0, xs)
    y = jnp.moveaxis(ys, 0, 1)
    mean = jnp.mean(y, axis=-1, keepdims=True)
    var = jnp.mean(jnp.square(y - mean), axis=-1, keepdims=True)
    y = ((y - mean) * lax.rsqrt(var + GN_EPS)).reshape(B, S, RWKV_WIDTH).astype(p.dtype) * lnx_g + lnx_b
    bonus = jnp.sum(r * k * r_k, axis=-1, keepdims=True) * v
    return (y + bonus.reshape(B, S, RWKV_WIDTH)) * g


def _peer(h, wq, subkeys, U, V):
    B, S, D = h.shape
    T = B * S
    hf = h.reshape(T, D)
    q = (hf @ wq).reshape(T, PEER_HEADS, 2, PEER_HALF)
    s = jnp.einsum('thcd,hcnd->thcn', q, subkeys).astype(jnp.float32)
    sv, si = lax.top_k(s, PEER_TOPK)
    cand = sv[:, :, 0, :, None] + sv[:, :, 1, None, :]
    cidx = si[:, :, 0, :, None] * PEER_NKEYS + si[:, :, 1, None, :]
    fv, fpos = lax.top_k(cand.reshape(T, PEER_HEADS, PEER_TOPK * PEER_TOPK), PEER_TOPK)
    fidx = jnp.take_along_axis(cidx.reshape(T, PEER_HEADS, PEER_TOPK * PEER_TOPK), fpos, axis=-1)
    gates = jax.nn.softmax(fv, axis=-1).astype(h.dtype)

    def chunk(args):
        xc, ic, gc = args
        u = jnp.take(U, ic, axis=0)
        act = jax.nn.gelu(jnp.einsum('chkd,cd->chk', u, xc), approximate=False)
        vv = jnp.take(V, ic, axis=0)
        return jnp.einsum('chk,chkd->cd', gc * act, vv)

    nc = T // PEER_CHUNK
    y = lax.map(chunk, (hf.reshape(nc, PEER_CHUNK, D),
                        fidx.reshape(nc, PEER_CHUNK, PEER_HEADS, PEER_TOPK),
                        gates.reshape(nc, PEER_CHUNK, PEER_HEADS, PEER_TOPK)))
    return y.reshape(B, S, D)


def setup_inputs(seed: int = 0) -> dict:
    key = jax.random.key(seed)
    ks = jax.random.split(key, 32)
    L, D = DEPTH, D_MODEL
    beta = (8.0 * DEPTH) ** -0.25
    nrm = lambda i, shape, sc: jax.random.normal(ks[i], shape, jnp.float32) * sc
    uni = lambda i, shape, lo, hi: jax.random.uniform(ks[i], shape, jnp.float32, lo, hi)
    return {
        'x': nrm(0, (BATCH, SEQ, D), 1.0),
        'c': nrm(1, (BATCH, D), 1.0),
        'ada_w': nrm(2, (L, D, 6 * D), 0.5 * D ** -0.5),
        'ada_b': nrm(3, (L, 6 * D), 0.02),
        'w_in': nrm(4, (L, D, IN_COLS), D ** -0.5),
        'attn_sink': nrm(5, (L, ATTN_HEADS), 1.0),
        'shift_mu': uni(6, (L, RWKV_COLS), 0.0, 1.0),
        'decay_w0': uni(7, (L, RWKV_WIDTH), -6.0, 1.0),
        'decay_w2': nrm(8, (L, DECAY_LORA, RWKV_WIDTH), 0.5 * DECAY_LORA ** -0.5),
        'iclr_a0': nrm(9, (L, RWKV_WIDTH), 0.5),
        'iclr_a2': nrm(10, (L, ICLR_LORA, RWKV_WIDTH), 0.5 * ICLR_LORA ** -0.5),
        'gate_g2': nrm(11, (L, GATE_LORA, RWKV_WIDTH), GATE_LORA ** -0.5),
        'k_k': 0.85 + nrm(12, (L, RWKV_WIDTH), 0.05),
        'k_a': 1.0 + nrm(13, (L, RWKV_WIDTH), 0.05),
        'r_k': nrm(14, (L, RWKV_HEADS, HEAD_DIM), 0.1),
        'lnx_g': 1.0 + nrm(15, (L, RWKV_WIDTH), 0.05),
        'lnx_b': nrm(16, (L, RWKV_WIDTH), 0.02),
        'w_out': nrm(17, (L, MIX_WIDTH, D), beta * MIX_WIDTH ** -0.5),
        'ln1_g': 1.0 + nrm(18, (L, D), 0.05),
        'ln1_b': nrm(19, (L, D), 0.02),
        'peer_wq': nrm(20, (L, D, PEER_HEADS * PEER_QDIM), D ** -0.5),
        'peer_subkeys': nrm(21, (L, PEER_HEADS, 2, PEER_NKEYS, PEER_HALF), PEER_HALF ** -0.5),
        'peer_u': nrm(22, (L, PEER_EXPERTS, D), D ** -0.5),
        'peer_v': nrm(23, (L, PEER_EXPERTS, D), beta),
        'ln2_g': 1.0 + nrm(24, (L, D), 0.05),
        'ln2_b': nrm(25, (L, D), 0.02),
    }


def reference(x, c, ada_w, ada_b, w_in, attn_sink, shift_mu, decay_w0, decay_w2, iclr_a0, iclr_a2,
              gate_g2, k_k, k_a, r_k, lnx_g, lnx_b, w_out, ln1_g, ln1_b, peer_wq, peer_subkeys,
              peer_u, peer_v, ln2_g, ln2_b):
    B, S, D = x.shape
    alpha = (2.0 * DEPTH) ** 0.25
    cs = jax.nn.silu(c)
    for l in range(DEPTH):
        mod = cs @ ada_w[l] + ada_b[l]
        sh1, sc1, gt1, sh2, sc2, gt2 = [m[:, None, :] for m in jnp.split(mod, 6, axis=-1)]
        h = x * (1.0 + sc1) + sh1
        proj = h @ w_in[l]
        q = proj[..., :ATTN_WIDTH].reshape(B, S, ATTN_HEADS, HEAD_DIM)
        k = proj[..., ATTN_WIDTH:ATTN_WIDTH + KV_WIDTH].reshape(B, S, ATTN_KV_HEADS, HEAD_DIM)
        v = proj[..., ATTN_WIDTH + KV_WIDTH:ATTN_COLS].reshape(B, S, ATTN_KV_HEADS, HEAD_DIM)
        y_attn = _sliding_window_sink_attention(q, k, v, attn_sink[l])
        y_rwkv = _rwkv7_time_mix(proj[..., ATTN_COLS:], shift_mu[l], decay_w0[l], decay_w2[l], iclr_a0[l],
                                 iclr_a2[l], gate_g2[l], k_k[l], k_a[l], r_k[l], lnx_g[l], lnx_b[l])
        y = jnp.concatenate([y_attn, y_rwkv], axis=-1) @ w_out[l]
        x = _layer_norm(alpha * x + gt1 * y, ln1_g[l], ln1_b[l])
        h = x * (1.0 + sc2) + sh2
        y = _peer(h, peer_wq[l], peer_subkeys[l], peer_u[l], peer_v[l])
        x = _layer_norm(alpha * x + gt2 * y, ln2_g[l], ln2_b[l])
    return x
```

```python
import functools
import math

import jax
import jax.numpy as jnp
from jax import lax
from jax.experimental import pallas as pl
from jax.experimental.pallas import tpu as pltpu

F32 = jnp.float32
BF16 = jnp.bfloat16

HEAD_DIM = 64
WINDOW = 128
TOPK = 16
LN_EPS = 1e-5
GN_EPS = 64e-5
LANES = 128
SUBLANES = 8
VMEM_LIMIT = 48 * 1024 * 1024
NEG = -1e30


def _cparams(n_axes, vmem=VMEM_LIMIT):
    return pltpu.CompilerParams(dimension_semantics=("arbitrary",) * n_axes, vmem_limit_bytes=vmem)


def _split3(x):
    hi = x.astype(BF16)
    r1 = x - hi.astype(F32)
    mid = r1.astype(BF16)
    lo = (r1 - mid.astype(F32)).astype(BF16)
    return hi, mid, lo


def _dot(a, b):
    return jnp.dot(a, b, preferred_element_type=F32)


def _dot_nt(a, b):
    return lax.dot_general(a, b, (((1,), (1,)), ((), ())), preferred_element_type=F32)


def _dot3(a, b):
    ah = a.astype(BF16)
    al = (a - ah.astype(F32)).astype(BF16)
    bh = b.astype(BF16)
    bl = (b - bh.astype(F32)).astype(BF16)
    return _dot(ah, bh) + _dot(ah, bl) + _dot(al, bh)


def _dot_exact_rhs(x, m_bf16):
    hi, mid, lo = _split3(x)
    return _dot(hi, m_bf16) + _dot(mid, m_bf16) + _dot(lo, m_bf16)


def _dot_exact_lhs(m_bf16, x):
    hi, mid, lo = _split3(x)
    return _dot(m_bf16, hi) + _dot(m_bf16, mid) + _dot(m_bf16, lo)


def _head_ones(width):
    r = lax.broadcasted_iota(jnp.int32, (width, width), 0) // HEAD_DIM
    c = lax.broadcasted_iota(jnp.int32, (width, width), 1) // HEAD_DIM
    return jnp.where(r == c, 1.0, 0.0).astype(BF16)


def _ada_kernel(c_ref, w_ref, b_ref, o_ref):
    c = c_ref[...]
    cs = c * jax.nn.sigmoid(c)
    o_ref[...] = _dot3(cs, w_ref[...]) + b_ref[...]


def _ada(c, w, b):
    B, D = c.shape
    N = w.shape[1]
    tn = 1536 if N % 1536 == 0 else N
    return pl.pallas_call(
        _ada_kernel,
        name="ada",
        grid=(N // tn,),
        in_specs=[pl.BlockSpec((B, D), lambda j: (0, 0)),
                  pl.BlockSpec((D, tn), lambda j: (0, j)),
                  pl.BlockSpec((1, tn), lambda j: (0, j))],
        out_specs=pl.BlockSpec((B, tn), lambda j: (0, j)),
        out_shape=jax.ShapeDtypeStruct((B, N), F32),
        compiler_params=_cparams(1),
    )(c, w, b.reshape(1, N))


def _proj_kernel(x_ref, mod_ref, w_ref, q_ref, k_ref, v_ref, p_ref, *, aw, kw):
    m = mod_ref[0]
    h = x_ref[...] * (1.0 + m[1:2]) + m[0:1]
    p = _dot(h.astype(BF16), w_ref[...])
    q_ref[...] = (p[:, :aw] * (HEAD_DIM ** -0.5)).astype(BF16)
    k_ref[...] = p[:, aw:aw + kw].astype(BF16)
    v_ref[...] = p[:, aw + kw:aw + 2 * kw].astype(BF16)
    p_ref[...] = p[:, aw + 2 * kw:]


def _proj(x2, mod3, w_bf, S, aw, kw):
    T, D = x2.shape
    N = w_bf.shape[1]
    rw = N - aw - 2 * kw
    tm = min(512, S)
    tps = S // tm
    return pl.pallas_call(
        functools.partial(_proj_kernel, aw=aw, kw=kw),
        name="proj",
        grid=(T // tm,),
        in_specs=[pl.BlockSpec((tm, D), lambda i: (i, 0)),
                  pl.BlockSpec((1, 6, D), lambda i: (i // tps, 0, 0)),
                  pl.BlockSpec((D, N), lambda i: (0, 0))],
        out_specs=[pl.BlockSpec((tm, aw), lambda i: (i, 0)),
                   pl.BlockSpec((tm, kw), lambda i: (i, 0)),
                   pl.BlockSpec((tm, kw), lambda i: (i, 0)),
                   pl.BlockSpec((tm, rw), lambda i: (i, 0))],
        out_shape=[jax.ShapeDtypeStruct((T, aw), BF16),
                   jax.ShapeDtypeStruct((T, kw), BF16),
                   jax.ShapeDtypeStruct((T, kw), BF16),
                   jax.ShapeDtypeStruct((T, rw), F32)],
        compiler_params=_cparams(1),
    )(x2, mod3, w_bf)


def _attn_kernel(sink_ref, q_ref, kp_ref, kc_ref, vp_ref, vc_ref, o_ref, *, heads, kv_heads):
    n = pl.program_id(1)
    groups = heads // kv_heads
    qi = lax.broadcasted_iota(jnp.int32, (WINDOW, 2 * WINDOW), 0)
    kj = lax.broadcasted_iota(jnp.int32, (WINDOW, 2 * WINDOW), 1)
    mask = (kj > qi) & (kj <= qi + WINDOW) & ((kj >= WINDOW) | (n > 0))
    for kh in range(kv_heads):
        ks = slice(kh * HEAD_DIM, (kh + 1) * HEAD_DIM)
        kwin = jnp.concatenate([kp_ref[:, ks], kc_ref[:, ks]], axis=0)
        vwin = jnp.concatenate([vp_ref[:, ks], vc_ref[:, ks]], axis=0)
        for g in range(groups):
            h = kh * groups + g
            hs = slice(h * HEAD_DIM, (h + 1) * HEAD_DIM)
            s = _dot_nt(q_ref[:, hs], kwin)
            s = jnp.where(mask, s, NEG)
            sk = sink_ref[h]
            m = jnp.maximum(jnp.max(s, axis=1, keepdims=True), sk)
            p = jnp.exp(s - m)
            den = jnp.sum(p, axis=1, keepdims=True) + jnp.exp(sk - m)
            o = _dot(p.astype(BF16), vwin) / den
            o_ref[:, hs] = o.astype(o_ref.dtype)


def _attn(sink, q, k, v, B, S, heads, kv_heads):
    T, aw = q.shape
    kw = k.shape[1]
    nb = S // WINDOW
    cur = lambda b, n: (b * nb + n, 0)
    prev = lambda b, n: (b * nb + jnp.maximum(n - 1, 0), 0)
    return pl.pallas_call(
        functools.partial(_attn_kernel, heads=heads, kv_heads=kv_heads),
        name="attn",
        grid=(B, nb),
        in_specs=[pl.BlockSpec(memory_space=pltpu.SMEM),
                  pl.BlockSpec((WINDOW, aw), cur),
                  pl.BlockSpec((WINDOW, kw), prev),
                  pl.BlockSpec((WINDOW, kw), cur),
                  pl.BlockSpec((WINDOW, kw), prev),
                  pl.BlockSpec((WINDOW, kw), cur)],
        out_specs=pl.BlockSpec((WINDOW, aw), cur),
        out_shape=jax.ShapeDtypeStruct((T, aw), BF16),
        compiler_params=_cparams(2),
    )(sink, q, k, k, v, v)


def _softplus(z):
    return jnp.maximum(z, 0.0) + jnp.log1p(jnp.exp(-jnp.abs(z)))


def _prep_kernel(p_ref, prev_ref, mu_ref, w0_ref, w2_ref, a0_ref, a2_ref, g2_ref, kk_ref, ka_ref, rk_ref,
                 r_o, lw_o, k_o, v_o, kk_o, b_o, g_o, bonus_o, *, rw, dl, al):
    i = pl.program_id(1)
    p = p_ref[...]
    last = jnp.where(i > 0, prev_ref[SUBLANES - 1:SUBLANES, :], 0.0)
    row = lax.broadcasted_iota(jnp.int32, p.shape, 0)
    p_prev = jnp.where(row == 0, last, pltpu.roll(p, 1, 0))
    p = p + (p_prev - p) * mu_ref[...]
    r = p[:, 0:rw]
    k = p[:, rw:2 * rw]
    v = p[:, 2 * rw:3 * rw]
    wd = p[:, 3 * rw:3 * rw + dl]
    ad = p[:, 3 * rw + dl:3 * rw + dl + al]
    gd = p[:, 3 * rw + dl + al:]
    ones = _head_ones(rw)
    w_log = -_softplus(-(w0_ref[...] + _dot3(jnp.tanh(wd), w2_ref[...]))) - 0.5
    a = jax.nn.sigmoid(a0_ref[...] + _dot3(ad, a2_ref[...]))
    g = _dot3(jax.nn.sigmoid(gd), g2_ref[...])
    kk = k * kk_ref[...]
    kk = kk * lax.rsqrt(jnp.maximum(_dot_exact_rhs(kk * kk, ones), 1e-24))
    k = k * (1.0 + (a - 1.0) * ka_ref[...])
    r_o[...] = r
    lw_o[...] = -jnp.exp(w_log)
    k_o[...] = k
    v_o[...] = v
    kk_o[...] = kk
    b_o[...] = kk * a
    g_o[...] = g
    bonus_o[...] = _dot_exact_rhs(r * k * rk_ref[...], ones) * v


def _prep(pr, S, mu, w0, w2, a0, a2, g2, k_k, k_a, r_k):
    T, cols = pr.shape
    rw = w0.shape[-1]
    dl, al = w2.shape[0], a2.shape[0]
    tt = min(256, S)
    tps = S // tt
    row = lambda a: a.reshape(1, -1)
    full = lambda a: pl.BlockSpec(a.shape, lambda b, i: (0,) * a.ndim)
    tile = lambda b, i: (b * tps + i, 0)
    prev = lambda b, i: (jnp.maximum((b * tps + i) * (tt // SUBLANES) - 1, 0), 0)
    args = [row(mu), row(w0), w2, row(a0), a2, g2, row(k_k), row(k_a), row(r_k)]
    out = jax.ShapeDtypeStruct((T, rw), F32)
    return pl.pallas_call(
        functools.partial(_prep_kernel, rw=rw, dl=dl, al=al),
        name="prep",
        grid=(T // S, tps),
        in_specs=[pl.BlockSpec((tt, cols), tile), pl.BlockSpec((SUBLANES, cols), prev)] + [full(a) for a in args],
        out_specs=[pl.BlockSpec((tt, rw), tile)] * 8,
        out_shape=[out] * 8,
        compiler_params=_cparams(2),
    )(pr, pr, *args)


CHUNK = 64


def _scan_kernel(r_ref, lw_ref, k_ref, v_ref, kk_ref, b_ref, y_ref, st_ref, *, heads):
    C = CHUNK

    @pl.when(pl.program_id(1) == 0)
    def _():
        st_ref[...] = jnp.zeros_like(st_ref)

    ri = lax.broadcasted_iota(jnp.int32, (C, C), 0)
    ci = lax.broadcasted_iota(jnp.int32, (C, C), 1)
    incl = ri >= ci
    strict = ri > ci
    lw = lw_ref[...]
    cum = _dot_exact_lhs(jnp.where(incl, 1.0, 0.0).astype(BF16), lw)
    pe = jnp.exp(cum)
    pinv = jnp.exp(-cum)
    rt = (r_ref[...] * pe).astype(BF16)
    nt = (-kk_ref[...] * jnp.exp(cum - lw)).astype(BF16)
    bt = (b_ref[...] * pinv).astype(BF16)
    kt = (k_ref[...] * pinv).astype(BF16)
    vb = v_ref[...].astype(BF16)
    for h in range(heads):
        hs = slice(h * HEAD_DIM, (h + 1) * HEAD_DIM)
        nr = jnp.concatenate([nt[:, hs], rt[:, hs]], axis=0)
        bh, kh, vh = bt[:, hs], kt[:, hs], vb[:, hs]
        ab = _dot_nt(nr, bh)
        ak = _dot_nt(nr, kh)
        s0 = st_ref[h]
        ws = _dot_nt(nr, s0.astype(BF16))
        u = ws[:C] + _dot(jnp.where(strict, ak[:C], 0.0).astype(BF16), vh)
        ap = jnp.where(strict, ab[:C], 0.0)
        steps = int(math.log2(C))
        for i in range(steps):
            apb = ap.astype(BF16)
            u = u + _dot(apb, u.astype(BF16))
            if i + 1 < steps:
                ap = _dot(apb, apb)
        ub = u.astype(BF16)
        y = (ws[C:] + _dot(jnp.where(incl, ab[C:], 0.0).astype(BF16), ub)
             + _dot(jnp.where(incl, ak[C:], 0.0).astype(BF16), vh))
        y_ref[:, hs] = y
        uvt = jnp.concatenate([u, v_ref[:, hs]], axis=0).T.astype(BF16)
        bkt = jnp.concatenate([bh, kh], axis=0)
        st_ref[h] = (s0 + _dot(uvt, bkt)) * pe[C - 1:C, hs]


def _scan(r, lw, k, v, kk, b, S, heads):
    T, rw = r.shape
    nch = S // CHUNK
    tile = pl.BlockSpec((CHUNK, rw), lambda bb, c: (bb * nch + c, 0))
    return pl.pallas_call(
        functools.partial(_scan_kernel, heads=heads),
        name="scan",
        grid=(T // S, nch),
        in_specs=[tile] * 6,
        out_specs=tile,
        out_shape=jax.ShapeDtypeStruct((T, rw), F32),
        scratch_shapes=[pltpu.VMEM((heads, HEAD_DIM, HEAD_DIM), F32)],
        compiler_params=_cparams(2),
    )(r, lw, k, v, kk, b)


def _layer_norm(z, g, b):
    mu = jnp.mean(z, axis=-1, keepdims=True)
    d = z - mu
    var = jnp.mean(d * d, axis=-1, keepdims=True)
    return d * lax.rsqrt(var + LN_EPS) * g + b


def _post_kernel(ya_ref, ys_ref, bonus_ref, g_ref, x_ref, mod_ref, lg_ref, lb_ref, wo_ref, n1g_ref, n1b_ref,
                 x1_o, h2_o, *, alpha):
    m = mod_ref[0]
    ys = ys_ref[...]
    rw = ys.shape[1]
    aw = ya_ref.shape[1]
    ones = _head_ones(rw)
    mean = _dot_exact_rhs(ys, ones) * (1.0 / HEAD_DIM)
    d = ys - mean
    var = _dot_exact_rhs(d * d, ones) * (1.0 / HEAD_DIM)
    yr = (d * lax.rsqrt(var + GN_EPS) * lg_ref[...] + lb_ref[...] + bonus_ref[...]) * g_ref[...]
    y = _dot(ya_ref[...], wo_ref[0:aw, :]) + _dot(yr.astype(BF16), wo_ref[aw:aw + rw, :])
    x1 = _layer_norm(alpha * x_ref[...] + m[2:3] * y, n1g_ref[...], n1b_ref[...])
    x1_o[...] = x1
    h2_o[...] = x1 * (1.0 + m[4:5]) + m[3:4]


def _post(ya, ys, bonus, g, x2, mod3, lnx_g, lnx_b, wo_bf, ln1_g, ln1_b, S, alpha):
    T, D = x2.shape
    aw, rw = ya.shape[1], ys.shape[1]
    tt = min(256, S)
    tps = S // tt
    row = lambda a: a.reshape(1, -1)
    full = lambda a: pl.BlockSpec(a.shape, lambda i: (0,) * a.ndim)
    tile = lambda w: pl.BlockSpec((tt, w), lambda i: (i, 0))
    consts = [row(lnx_g), row(lnx_b), wo_bf, row(ln1_g), row(ln1_b)]
    return pl.pallas_call(
        functools.partial(_post_kernel, alpha=alpha),
        name="post",
        grid=(T // tt,),
        in_specs=[tile(aw), tile(rw), tile(rw), tile(rw), tile(D),
                  pl.BlockSpec((1, 6, D), lambda i: (i // tps, 0, 0))] + [full(a) for a in consts],
        out_specs=[tile(D), tile(D)],
        out_shape=[jax.ShapeDtypeStruct((T, D), F32)] * 2,
        compiler_params=_cparams(1),
    )(ya, ys, bonus, g, x2, mod3, *consts)


def _top_rows(s, payload, count):
    rows = s.shape[0]
    iota = lax.broadcasted_iota(jnp.int32, s.shape, 0)
    vals, pays = [], []
    for _ in range(count):
        m = jnp.max(s, axis=0, keepdims=True)
        ix = jnp.min(jnp.where(s == m, iota, rows), axis=0, keepdims=True)
        hit = iota == ix
        vals.append(m)
        pays.append(jnp.sum(jnp.where(hit, payload, 0), axis=0, keepdims=True))
        s = jnp.where(hit, -jnp.inf, s)
    return vals, pays


def _pkeys_kernel(h_ref, wq_ref, sk_ref, row_o, sh_o, gate_o, *, heads, nkeys, half):
    q = _dot(h_ref[...].astype(BF16), wq_ref[...])
    L = q.shape[0]
    key_iota = lax.broadcasted_iota(jnp.int32, (nkeys, L), 0)
    j8 = lax.broadcasted_iota(jnp.int32, (SUBLANES, L), 0)
    ninf = -jnp.inf
    for h in range(heads):
        sv, si = [], []
        for c in range(2):
            qc = q[:, (2 * h + c) * half:(2 * h + c + 1) * half].astype(BF16)
            st = _dot_nt(sk_ref[2 * h + c], qc)
            v_, i_ = _top_rows(st, key_iota, TOPK)
            sv.append(v_)
            si.append(i_)
        sv1 = jnp.concatenate(sv[1], axis=0)
        si1 = jnp.concatenate(si[1], axis=0)
        cv = [sv[0][0] + sv1]
        ce = [si[0][0] * nkeys + si1]
        for i in range(1, SUBLANES):
            cv.append(jnp.where(j8 < TOPK // (i + 1), sv[0][i] + sv1[:SUBLANES], ninf))
            ce.append(si[0][i] * nkeys + si1[:SUBLANES])
        cv.append(jnp.concatenate(sv[0][SUBLANES:], axis=0) + sv1[0:1])
        ce.append(jnp.concatenate(si[0][SUBLANES:], axis=0) * nkeys + si1[0:1])
        fv, fe = _top_rows(jnp.concatenate(cv, axis=0), jnp.concatenate(ce, axis=0), TOPK)
        fv = jnp.concatenate(fv, axis=0)
        fe = jnp.concatenate(fe, axis=0)
        e = jnp.exp(fv - fv[0:1])
        hs = slice(h * TOPK, (h + 1) * TOPK)
        gate_o[hs, :] = e / jnp.sum(e, axis=0, keepdims=True)
        pairs = nkeys * nkeys // 2
        row_o[hs, :] = jnp.where(fe >= pairs, fe - pairs, fe)
        sh_o[hs, :] = jnp.where(fe >= pairs, 16, 0)


def _pkeys(h2, wq_bf, sk_bf, heads, nkeys, half):
    T, D = h2.shape
    tt = min(256, T)
    n_out = heads * TOPK
    out_spec = pl.BlockSpec((n_out, tt), lambda i: (0, i))
    return pl.pallas_call(
        functools.partial(_pkeys_kernel, heads=heads, nkeys=nkeys, half=half),
        name="pkeys",
        grid=(T // tt,),
        in_specs=[pl.BlockSpec((tt, D), lambda i: (i, 0)),
                  pl.BlockSpec(wq_bf.shape, lambda i: (0, 0)),
                  pl.BlockSpec(sk_bf.shape, lambda i: (0, 0, 0))],
        out_specs=[out_spec] * 3,
        out_shape=[jax.ShapeDtypeStruct((n_out, T), jnp.int32),
                   jax.ShapeDtypeStruct((n_out, T), jnp.int32),
                   jax.ShapeDtypeStruct((n_out, T), F32)],
        compiler_params=_cparams(1),
    )(h2, wq_bf, sk_bf)


PEER_TOK = 64


def _pack_table(tab):
    n, d = tab.shape
    bits = lax.bitcast_convert_type(tab.astype(BF16), jnp.uint16).astype(jnp.uint32)
    packed = (bits[:n // 2] << 16) | bits[n // 2:]
    return packed.reshape(n // 2, d // LANES, LANES)


def _expert_row(tab_ref, row, shift):
    w = tab_ref[row]
    bits = (w << shift.astype(jnp.uint32)) & jnp.uint32(0xFFFF0000)
    return lax.bitcast_convert_type(bits, F32)


def _peer_u_kernel(row_ref, sh_ref, h_ref, gate_ref, tab_ref, o_ref, part_ref, *, n_sel):
    ones = jnp.ones((SUBLANES, LANES), BF16)

    def token(t, carry):
        h = h_ref[t]
        for k in range(n_sel):
            u = _expert_row(tab_ref, row_ref[t, k], sh_ref[t, k])
            part_ref[k:k + 1, :] = jnp.sum(u * h, axis=0, keepdims=True)
        hi, mid, lo = _split3(part_ref[...])
        s = (_dot_nt(ones, hi) + _dot_nt(ones, mid) + _dot_nt(ones, lo))[0:1]
        act = s * (lax.erf(s * (2.0 ** -0.5)) + 1.0) * 0.5
        o_ref[pl.ds(t, 1), :] = act * gate_ref[pl.ds(t, 1), :]
        return carry

    lax.fori_loop(0, h_ref.shape[0], token, 0)


def _peer_v_kernel(row_ref, sh_ref, coef_ref, tab_ref, o_ref, *, n_sel):
    n_acc = 4

    def token(t, carry):
        acc = [None] * n_acc
        for k in range(n_sel):
            term = coef_ref[t, k] * _expert_row(tab_ref, row_ref[t, k], sh_ref[t, k])
            acc[k % n_acc] = term if acc[k % n_acc] is None else acc[k % n_acc] + term
        o_ref[t] = (acc[0] + acc[1]) + (acc[2] + acc[3])
        return carry

    lax.fori_loop(0, o_ref.shape[0], token, 0)


def _table_spec(tab):
    return pl.BlockSpec(tab.shape, lambda i: (0, 0, 0), pipeline_mode=pl.Buffered(1))


def _smem_tile(width):
    return pl.BlockSpec((PEER_TOK, width), lambda i: (i, 0), memory_space=pltpu.SMEM)


def _peer_u(rows, shifts, h3, gates, tab):
    T, n_sel = rows.shape
    dd = h3.shape[1]
    tt = min(PEER_TOK, T)
    return pl.pallas_call(
        functools.partial(_peer_u_kernel, n_sel=n_sel),
        name="peer_u",
        grid=(T // tt,),
        in_specs=[_smem_tile(n_sel), _smem_tile(n_sel),
                  pl.BlockSpec((tt, dd, LANES), lambda i: (i, 0, 0)),
                  pl.BlockSpec((tt, n_sel), lambda i: (i, 0)),
                  _table_spec(tab)],
        out_specs=pl.BlockSpec((tt, n_sel), lambda i: (i, 0)),
        out_shape=jax.ShapeDtypeStruct((T, n_sel), F32),
        scratch_shapes=[pltpu.VMEM((n_sel, LANES), F32)],
        compiler_params=_cparams(1),
    )(rows, shifts, h3, gates, tab)


def _peer_v(rows, shifts, coef, tab):
    T, n_sel = rows.shape
    dd = tab.shape[1]
    tt = min(PEER_TOK, T)
    return pl.pallas_call(
        functools.partial(_peer_v_kernel, n_sel=n_sel),
        name="peer_v",
        grid=(T // tt,),
        in_specs=[_smem_tile(n_sel), _smem_tile(n_sel), _smem_tile(n_sel), _table_spec(tab)],
        out_specs=pl.BlockSpec((tt, dd, LANES), lambda i: (i, 0, 0)),
        out_shape=jax.ShapeDtypeStruct((T, dd, LANES), F32),
        compiler_params=_cparams(1),
    )(rows, shifts, coef, tab)


def _ln2_kernel(x1_ref, y_ref, mod_ref, g_ref, b_ref, o_ref, *, alpha):
    m = mod_ref[0]
    o_ref[...] = _layer_norm(alpha * x1_ref[...] + m[5:6] * y_ref[...], g_ref[...], b_ref[...])


def _ln2(x1, y, mod3, g, b, S, alpha):
    T, D = x1.shape
    tt = min(512, S)
    tps = S // tt
    tile = pl.BlockSpec((tt, D), lambda i: (i, 0))
    vec = pl.BlockSpec((1, D), lambda i: (0, 0))
    return pl.pallas_call(
        functools.partial(_ln2_kernel, alpha=alpha),
        name="ln2",
        grid=(T // tt,),
        in_specs=[tile, tile, pl.BlockSpec((1, 6, D), lambda i: (i // tps, 0, 0)), vec, vec],
        out_specs=tile,
        out_shape=jax.ShapeDtypeStruct((T, D), F32),
        compiler_params=_cparams(1),
    )(x1, y, mod3, g.reshape(1, D), b.reshape(1, D))


def kernel(x, c, ada_w, ada_b, w_in, attn_sink, shift_mu, decay_w0, decay_w2, iclr_a0, iclr_a2, gate_g2, k_k, k_a,
           r_k, lnx_g, lnx_b, w_out, ln1_g, ln1_b, peer_wq, peer_subkeys, peer_u, peer_v, ln2_g, ln2_b):
    B, S, D = x.shape
    depth = ada_w.shape[0]
    alpha = (2.0 * depth) ** 0.25
    heads = attn_sink.shape[1]
    rheads = r_k.shape[1]
    rw = rheads * HEAD_DIM
    aw = heads * HEAD_DIM
    kw = (w_in.shape[2] - aw - shift_mu.shape[1]) // 2
    kv_heads = kw // HEAD_DIM
    pheads, _, nkeys, half = peer_subkeys.shape[1:]
    assert S % WINDOW == 0 and S % CHUNK == 0 and (B * S) % PEER_TOK == 0
    T = B * S
    xt = x.reshape(T, D)
    for l in range(depth):
        mod3 = _ada(c, ada_w[l], ada_b[l]).reshape(B, 6, D)
        q, k, v, pr = _proj(xt, mod3, w_in[l].astype(BF16), S, aw, kw)
        ya = _attn(attn_sink[l], q, k, v, B, S, heads, kv_heads)
        r, lw, kr, vr, kk, bb, g, bonus = _prep(pr, S, shift_mu[l], decay_w0[l], decay_w2[l], iclr_a0[l],
                                                iclr_a2[l], gate_g2[l], k_k[l], k_a[l], r_k[l].reshape(-1))
        ys = _scan(r, lw, kr, vr, kk, bb, S, rheads)
        x1, h2 = _post(ya, ys, bonus, g, xt, mod3, lnx_g[l], lnx_b[l], w_out[l].astype(BF16), ln1_g[l], ln1_b[l],
                       S, alpha)
        rows_t, sh_t, gate_t = _pkeys(h2, peer_wq[l].astype(BF16),
                                      peer_subkeys[l].reshape(pheads * 2, nkeys, half).astype(BF16),
                                      pheads, nkeys, half)
        rows, shifts, gates = rows_t.T, sh_t.T, gate_t.T
        coef = _peer_u(rows, shifts, h2.reshape(T, D // LANES, LANES), gates, _pack_table(peer_u[l]))
        y = _peer_v(rows, shifts, coef, _pack_table(peer_v[l])).reshape(T, D)
        xt = _ln2(x1, y, mod3, ln2_g[l], ln2_b[l], S, alpha)
    return xt.reshape(B, S, D)
```

```python
import functools
import math

import jax
import jax.numpy as jnp
from jax import lax
from jax.experimental import pallas as pl
from jax.experimental.pallas import tpu as pltpu

F32 = jnp.float32
BF16 = jnp.bfloat16

HEAD_DIM = 64
WINDOW = 128
TOPK = 16
LN_EPS = 1e-5
GN_EPS = 64e-5
LANES = 128
SUBLANES = 8
VMEM_LIMIT = 48 * 1024 * 1024
NEG = -1e30


def _cparams(n_axes, vmem=VMEM_LIMIT):
    return pltpu.CompilerParams(dimension_semantics=("arbitrary",) * n_axes, vmem_limit_bytes=vmem)


def _split3(x):
    hi = x.astype(BF16)
    r1 = x - hi.astype(F32)
    mid = r1.astype(BF16)
    lo = (r1 - mid.astype(F32)).astype(BF16)
    return hi, mid, lo


def _dot(a, b):
    return jnp.dot(a, b, preferred_element_type=F32)


def _dot_nt(a, b):
    return lax.dot_general(a, b, (((1,), (1,)), ((), ())), preferred_element_type=F32)


def _dot3(a, b):
    ah = a.astype(BF16)
    al = (a - ah.astype(F32)).astype(BF16)
    bh = b.astype(BF16)
    bl = (b - bh.astype(F32)).astype(BF16)
    return _dot(ah, bh) + _dot(ah, bl) + _dot(al, bh)


def _dot_exact_rhs(x, m_bf16):
    hi, mid, lo = _split3(x)
    return _dot(hi, m_bf16) + _dot(mid, m_bf16) + _dot(lo, m_bf16)


def _dot_exact_lhs(m_bf16, x):
    hi, mid, lo = _split3(x)
    return _dot(m_bf16, hi) + _dot(m_bf16, mid) + _dot(m_bf16, lo)


def _head_ones(width):
    r = lax.broadcasted_iota(jnp.int32, (width, width), 0) // HEAD_DIM
    c = lax.broadcasted_iota(jnp.int32, (width, width), 1) // HEAD_DIM
    return jnp.where(r == c, 1.0, 0.0).astype(BF16)


def _ada_kernel(c_ref, w_ref, b_ref, o_ref):
    c = c_ref[...]
    cs = c * jax.nn.sigmoid(c)
    o_ref[...] = _dot3(cs, w_ref[...]) + b_ref[...]


def _ada(c, w, b):
    B, D = c.shape
    N = w.shape[1]
    tn = 1536 if N % 1536 == 0 else N
    return pl.pallas_call(
        _ada_kernel,
        name="ada",
        grid=(N // tn,),
        in_specs=[pl.BlockSpec((B, D), lambda j: (0, 0)),
                  pl.BlockSpec((D, tn), lambda j: (0, j)),
                  pl.BlockSpec((1, tn), lambda j: (0, j))],
        out_specs=pl.BlockSpec((B, tn), lambda j: (0, j)),
        out_shape=jax.ShapeDtypeStruct((B, N), F32),
        compiler_params=_cparams(1),
    )(c, w, b.reshape(1, N))


def _proj_kernel(x_ref, mod_ref, w_ref, q_ref, k_ref, v_ref, p_ref, *, aw, kw):
    m = mod_ref[0]
    h = x_ref[...] * (1.0 + m[1:2]) + m[0:1]
    p = _dot(h.astype(BF16), w_ref[...])
    q_ref[...] = (p[:, :aw] * (HEAD_DIM ** -0.5)).astype(BF16)
    k_ref[...] = p[:, aw:aw + kw].astype(BF16)
    v_ref[...] = p[:, aw + kw:aw + 2 * kw].astype(BF16)
    p_ref[...] = p[:, aw + 2 * kw:]


def _proj(x2, mod3, w_bf, S, aw, kw):
    T, D = x2.shape
    N = w_bf.shape[1]
    rw = N - aw - 2 * kw
    tm = min(512, S)
    tps = S // tm
    return pl.pallas_call(
        functools.partial(_proj_kernel, aw=aw, kw=kw),
        name="proj",
        grid=(T // tm,),
        in_specs=[pl.BlockSpec((tm, D), lambda i: (i, 0)),
                  pl.BlockSpec((1, 6, D), lambda i: (i // tps, 0, 0)),
                  pl.BlockSpec((D, N), lambda i: (0, 0))],
        out_specs=[pl.BlockSpec((tm, aw), lambda i: (i, 0)),
                   pl.BlockSpec((tm, kw), lambda i: (i, 0)),
                   pl.BlockSpec((tm, kw), lambda i: (i, 0)),
                   pl.BlockSpec((tm, rw), lambda i: (i, 0))],
        out_shape=[jax.ShapeDtypeStruct((T, aw), BF16),
                   jax.ShapeDtypeStruct((T, kw), BF16),
                   jax.ShapeDtypeStruct((T, kw), BF16),
                   jax.ShapeDtypeStruct((T, rw), F32)],
        compiler_params=_cparams(1),
    )(x2, mod3, w_bf)


def _attn_kernel(sink_ref, q_ref, kp_ref, kc_ref, vp_ref, vc_ref, o_ref, *, heads, kv_heads):
    n = pl.program_id(1)
    groups = heads // kv_heads
    qi = lax.broadcasted_iota(jnp.int32, (WINDOW, 2 * WINDOW), 0)
    kj = lax.broadcasted_iota(jnp.int32, (WINDOW, 2 * WINDOW), 1)
    mask = (kj > qi) & (kj <= qi + WINDOW) & ((kj >= WINDOW) | (n > 0))
    for kh in range(kv_heads):
        ks = slice(kh * HEAD_DIM, (kh + 1) * HEAD_DIM)
        kwin = jnp.concatenate([kp_ref[:, ks], kc_ref[:, ks]], axis=0)
        vwin = jnp.concatenate([vp_ref[:, ks], vc_ref[:, ks]], axis=0)
        for g in range(groups):
            h = kh * groups + g
            hs = slice(h * HEAD_DIM, (h + 1) * HEAD_DIM)
            s = _dot_nt(q_ref[:, hs], kwin)
            s = jnp.where(mask, s, NEG)
            sk = sink_ref[h]
            m = jnp.maximum(jnp.max(s, axis=1, keepdims=True), sk)
            p = jnp.exp(s - m)
            den = jnp.sum(p, axis=1, keepdims=True) + jnp.exp(sk - m)
            o = _dot(p.astype(BF16), vwin) / den
            o_ref[:, hs] = o.astype(o_ref.dtype)


def _attn(sink, q, k, v, B, S, heads, kv_heads):
    T, aw = q.shape
    kw = k.shape[1]
    nb = S // WINDOW
    cur = lambda b, n: (b * nb + n, 0)
    prev = lambda b, n: (b * nb + jnp.maximum(n - 1, 0), 0)
    return pl.pallas_call(
        functools.partial(_attn_kernel, heads=heads, kv_heads=kv_heads),
        name="attn",
        grid=(B, nb),
        in_specs=[pl.BlockSpec(memory_space=pltpu.SMEM),
                  pl.BlockSpec((WINDOW, aw), cur),
                  pl.BlockSpec((WINDOW, kw), prev),
                  pl.BlockSpec((WINDOW, kw), cur),
                  pl.BlockSpec((WINDOW, kw), prev),
                  pl.BlockSpec((WINDOW, kw), cur)],
        out_specs=pl.BlockSpec((WINDOW, aw), cur),
        out_shape=jax.ShapeDtypeStruct((T, aw), BF16),
        compiler_params=_cparams(2),
    )(sink, q, k, k, v, v)


def _softplus(z):
    return jnp.maximum(z, 0.0) + jnp.log1p(jnp.exp(-jnp.abs(z)))


def _prep_kernel(p_ref, prev_ref, mu_ref, w0_ref, w2_ref, a0_ref, a2_ref, g2_ref, kk_ref, ka_ref, rk_ref,
                 r_o, lw_o, k_o, v_o, kk_o, b_o, g_o, bonus_o, *, rw, dl, al):
    i = pl.program_id(1)
    p = p_ref[...]
    last = jnp.where(i > 0, prev_ref[SUBLANES - 1:SUBLANES, :], 0.0)
    row = lax.broadcasted_iota(jnp.int32, p.shape, 0)
    p_prev = jnp.where(row == 0, last, pltpu.roll(p, 1, 0))
    p = p + (p_prev - p) * mu_ref[...]
    r = p[:, 0:rw]
    k = p[:, rw:2 * rw]
    v = p[:, 2 * rw:3 * rw]
    wd = p[:, 3 * rw:3 * rw + dl]
    ad = p[:, 3 * rw + dl:3 * rw + dl + al]
    gd = p[:, 3 * rw + dl + al:]
    ones = _head_ones(rw)
    w_log = -_softplus(-(w0_ref[...] + _dot3(jnp.tanh(wd), w2_ref[...]))) - 0.5
    a = jax.nn.sigmoid(a0_ref[...] + _dot3(ad, a2_ref[...]))
    g = _dot3(jax.nn.sigmoid(gd), g2_ref[...])
    kk = k * kk_ref[...]
    kk = kk * lax.rsqrt(jnp.maximum(_dot_exact_rhs(kk * kk, ones), 1e-24))
    k = k * (1.0 + (a - 1.0) * ka_ref[...])
    r_o[...] = r
    lw_o[...] = -jnp.exp(w_log)
    k_o[...] = k
    v_o[...] = v
    kk_o[...] = kk
    b_o[...] = kk * a
    g_o[...] = g
    bonus_o[...] = _dot_exact_rhs(r * k * rk_ref[...], ones) * v


def _prep(pr, S, mu, w0, w2, a0, a2, g2, k_k, k_a, r_k):
    T, cols = pr.shape
    rw = w0.shape[-1]
    dl, al = w2.shape[0], a2.shape[0]
    tt = min(256, S)
    tps = S // tt
    row = lambda a: a.reshape(1, -1)
    full = lambda a: pl.BlockSpec(a.shape, lambda b, i: (0,) * a.ndim)
    tile = lambda b, i: (b * tps + i, 0)
    prev = lambda b, i: (jnp.maximum((b * tps + i) * (tt // SUBLANES) - 1, 0), 0)
    args = [row(mu), row(w0), w2, row(a0), a2, g2, row(k_k), row(k_a), row(r_k)]
    out = jax.ShapeDtypeStruct((T, rw), F32)
    return pl.pallas_call(
        functools.partial(_prep_kernel, rw=rw, dl=dl, al=al),
        name="prep",
        grid=(T // S, tps),
        in_specs=[pl.BlockSpec((tt, cols), tile), pl.BlockSpec((SUBLANES, cols), prev)] + [full(a) for a in args],
        out_specs=[pl.BlockSpec((tt, rw), tile)] * 8,
        out_shape=[out] * 8,
        compiler_params=_cparams(2),
    )(pr, pr, *args)


CHUNK = 64


def _scan_kernel(r_ref, lw_ref, k_ref, v_ref, kk_ref, b_ref, y_ref, st_ref, *, heads):
    C = CHUNK

    @pl.when(pl.program_id(1) == 0)
    def _():
        st_ref[...] = jnp.zeros_like(st_ref)

    ri = lax.broadcasted_iota(jnp.int32, (C, C), 0)
    ci = lax.broadcasted_iota(jnp.int32, (C, C), 1)
    incl = ri >= ci
    strict = ri > ci
    lw = lw_ref[...]
    cum = _dot_exact_lhs(jnp.where(incl, 1.0, 0.0).astype(BF16), lw)
    pe = jnp.exp(cum)
    pinv = jnp.exp(-cum)
    rt = (r_ref[...] * pe).astype(BF16)
    nt = (-kk_ref[...] * jnp.exp(cum - lw)).astype(BF16)
    bt = (b_ref[...] * pinv).astype(BF16)
    kt = (k_ref[...] * pinv).astype(BF16)
    vb = v_ref[...].astype(BF16)
    for h in range(heads):
        hs = slice(h * HEAD_DIM, (h + 1) * HEAD_DIM)
        nr = jnp.concatenate([nt[:, hs], rt[:, hs]], axis=0)
        bh, kh, vh = bt[:, hs], kt[:, hs], vb[:, hs]
        ab = _dot_nt(nr, bh)
        ak = _dot_nt(nr, kh)
        s0 = st_ref[h]
        ws = _dot_nt(nr, s0.astype(BF16))
        u = ws[:C] + _dot(jnp.where(strict, ak[:C], 0.0).astype(BF16), vh)
        ap = jnp.where(strict, ab[:C], 0.0)
        steps = int(math.log2(C))
        for i in range(steps):
            apb = ap.astype(BF16)
            u = u + _dot(apb, u.astype(BF16))
            if i + 1 < steps:
                ap = _dot(apb, apb)
        ub = u.astype(BF16)
        y = (ws[C:] + _dot(jnp.where(incl, ab[C:], 0.0).astype(BF16), ub)
             + _dot(jnp.where(incl, ak[C:], 0.0).astype(BF16), vh))
        y_ref[:, hs] = y
        uvt = jnp.concatenate([u, v_ref[:, hs]], axis=0).T.astype(BF16)
        bkt = jnp.concatenate([bh, kh], axis=0)
        st_ref[h] = (s0 + _dot(uvt, bkt)) * pe[C - 1:C, hs]


def _scan(r, lw, k, v, kk, b, S, heads):
    T, rw = r.shape
    nch = S // CHUNK
    tile = pl.BlockSpec((CHUNK, rw), lambda bb, c: (bb * nch + c, 0))
    return pl.pallas_call(
        functools.partial(_scan_kernel, heads=heads),
        name="scan",
        grid=(T // S, nch),
        in_specs=[tile] * 6,
        out_specs=tile,
        out_shape=jax.ShapeDtypeStruct((T, rw), F32),
        scratch_shapes=[pltpu.VMEM((heads, HEAD_DIM, HEAD_DIM), F32)],
        compiler_params=_cparams(2),
    )(r, lw, k, v, kk, b)


def _layer_norm(z, g, b):
    mu = jnp.mean(z, axis=-1, keepdims=True)
    d = z - mu
    var = jnp.mean(d * d, axis=-1, keepdims=True)
    return d * lax.rsqrt(var + LN_EPS) * g + b


def _post_kernel(ya_ref, ys_ref, bonus_ref, g_ref, x_ref, mod_ref, lg_ref, lb_ref, wo_ref, n1g_ref, n1b_ref,
                 x1_o, h2_o, *, alpha):
    m = mod_ref[0]
    ys = ys_ref[...]
    rw = ys.shape[1]
    aw = ya_ref.shape[1]
    ones = _head_ones(rw)
    mean = _dot_exact_rhs(ys, ones) * (1.0 / HEAD_DIM)
    d = ys - mean
    var = _dot_exact_rhs(d * d, ones) * (1.0 / HEAD_DIM)
    yr = (d * lax.rsqrt(var + GN_EPS) * lg_ref[...] + lb_ref[...] + bonus_ref[...]) * g_ref[...]
    y = _dot(ya_ref[...], wo_ref[0:aw, :]) + _dot(yr.astype(BF16), wo_ref[aw:aw + rw, :])
    x1 = _layer_norm(alpha * x_ref[...] + m[2:3] * y, n1g_ref[...], n1b_ref[...])
    x1_o[...] = x1
    h2_o[...] = x1 * (1.0 + m[4:5]) + m[3:4]


def _post(ya, ys, bonus, g, x2, mod3, lnx_g, lnx_b, wo_bf, ln1_g, ln1_b, S, alpha):
    T, D = x2.shape
    aw, rw = ya.shape[1], ys.shape[1]
    tt = min(256, S)
    tps = S // tt
    row = lambda a: a.reshape(1, -1)
    full = lambda a: pl.BlockSpec(a.shape, lambda i: (0,) * a.ndim)
    tile = lambda w: pl.BlockSpec((tt, w), lambda i: (i, 0))
    consts = [row(lnx_g), row(lnx_b), wo_bf, row(ln1_g), row(ln1_b)]
    return pl.pallas_call(
        functools.partial(_post_kernel, alpha=alpha),
        name="post",
        grid=(T // tt,),
        in_specs=[tile(aw), tile(rw), tile(rw), tile(rw), tile(D),
                  pl.BlockSpec((1, 6, D), lambda i: (i // tps, 0, 0))] + [full(a) for a in consts],
        out_specs=[tile(D), tile(D)],
        out_shape=[jax.ShapeDtypeStruct((T, D), F32)] * 2,
        compiler_params=_cparams(1),
    )(ya, ys, bonus, g, x2, mod3, *consts)


def _top_rows(s, payload, count):
    rows = s.shape[0]
    iota = lax.broadcasted_iota(jnp.int32, s.shape, 0)
    vals, pays = [], []
    for _ in range(count):
        m = jnp.max(s, axis=0, keepdims=True)
        ix = jnp.min(jnp.where(s == m, iota, rows), axis=0, keepdims=True)
        hit = iota == ix
        vals.append(m)
        pays.append(jnp.sum(jnp.where(hit, payload, 0), axis=0, keepdims=True))
        s = jnp.where(hit, -jnp.inf, s)
    return vals, pays


def _pkeys_kernel(h_ref, wq_ref, sk_ref, row_o, sh_o, gate_o, *, heads, nkeys, half):
    q = _dot(h_ref[...].astype(BF16), wq_ref[...])
    L = q.shape[0]
    key_iota = lax.broadcasted_iota(jnp.int32, (nkeys, L), 0)
    j8 = lax.broadcasted_iota(jnp.int32, (SUBLANES, L), 0)
    ninf = -jnp.inf
    for h in range(heads):
        sv, si = [], []
        for c in range(2):
            qc = q[:, (2 * h + c) * half:(2 * h + c + 1) * half].astype(BF16)
            st = _dot_nt(sk_ref[2 * h + c], qc)
            v_, i_ = _top_rows(st, key_iota, TOPK)
            sv.append(v_)
            si.append(i_)
        sv1 = jnp.concatenate(sv[1], axis=0)
        si1 = jnp.concatenate(si[1], axis=0)
        cv = [sv[0][0] + sv1]
        ce = [si[0][0] * nkeys + si1]
        for i in range(1, SUBLANES):
            cv.append(jnp.where(j8 < TOPK // (i + 1), sv[0][i] + sv1[:SUBLANES], ninf))
            ce.append(si[0][i] * nkeys + si1[:SUBLANES])
        cv.append(jnp.concatenate(sv[0][SUBLANES:], axis=0) + sv1[0:1])
        ce.append(jnp.concatenate(si[0][SUBLANES:], axis=0) * nkeys + si1[0:1])
        fv, fe = _top_rows(jnp.concatenate(cv, axis=0), jnp.concatenate(ce, axis=0), TOPK)
        fv = jnp.concatenate(fv, axis=0)
        fe = jnp.concatenate(fe, axis=0)
        e = jnp.exp(fv - fv[0:1])
        hs = slice(h * TOPK, (h + 1) * TOPK)
        gate_o[hs, :] = e / jnp.sum(e, axis=0, keepdims=True)
        pairs = nkeys * nkeys // 2
        row_o[hs, :] = jnp.where(fe >= pairs, fe - pairs, fe) * SUBLANES
        sh_o[hs, :] = jnp.where(fe >= pairs, 16, 0)


def _pkeys(h2, wq_bf, sk_bf, heads, nkeys, half):
    T, D = h2.shape
    tt = min(256, T)
    n_out = heads * TOPK
    out_spec = pl.BlockSpec((n_out, tt), lambda i: (0, i))
    return pl.pallas_call(
        functools.partial(_pkeys_kernel, heads=heads, nkeys=nkeys, half=half),
        name="pkeys",
        grid=(T // tt,),
        in_specs=[pl.BlockSpec((tt, D), lambda i: (i, 0)),
                  pl.BlockSpec(wq_bf.shape, lambda i: (0, 0)),
                  pl.BlockSpec(sk_bf.shape, lambda i: (0, 0, 0))],
        out_specs=[out_spec] * 3,
        out_shape=[jax.ShapeDtypeStruct((n_out, T), jnp.int32),
                   jax.ShapeDtypeStruct((n_out, T), jnp.int32),
                   jax.ShapeDtypeStruct((n_out, T), F32)],
        compiler_params=_cparams(1),
    )(h2, wq_bf, sk_bf)


PEER_TOK = 128
PEER_UNROLL = 4


def _pack_table(tab):
    n, d = tab.shape
    assert d == SUBLANES * LANES
    bits = lax.bitcast_convert_type(tab.astype(BF16), jnp.uint16).astype(jnp.uint32)
    packed = (bits[:n // 2] << 16) | bits[n // 2:]
    return packed.reshape(n // 2 * SUBLANES, LANES)


GROUP = 32
SLOTS = 2 * SUBLANES


def _expand_matrix(n_sel):
    r = lax.broadcasted_iota(jnp.int32, (2 * n_sel, n_sel * SLOTS), 0)
    c = lax.broadcasted_iota(jnp.int32, (2 * n_sel, n_sel * SLOTS), 1)
    return jnp.where((r % n_sel == c // SLOTS) & (r // n_sel == c % 2), 1.0, 0.0).astype(BF16)


def _sublane_mask(rows, cols):
    r = lax.broadcasted_iota(jnp.int32, (rows, cols), 0)
    c = lax.broadcasted_iota(jnp.int32, (rows, cols), 1)
    return jnp.where(r % SUBLANES == (c % SLOTS) // 2, 1.0, 0.0).astype(F32)


def _gather_tiles(tab_ref, row_ref, t, first):
    half = GROUP // 2
    tile = lambda k: pltpu.bitcast(
        tab_ref[pl.ds(pl.multiple_of(row_ref[t, k], SUBLANES), SUBLANES), :], BF16)
    a = jnp.concatenate([tile(first + i) for i in range(half)], axis=0)
    b = jnp.concatenate([tile(first + half + i) for i in range(half)], axis=0)
    return jnp.concatenate([a, b], axis=1)


def _peer_u_kernel(row_ref, h_ref, gate_ref, sh_ref, et_ref, tab_ref, o_ref, z_ref, *, n_sel):
    gw = (GROUP // 2) * SLOTS
    m8 = _sublane_mask(SUBLANES, gw)

    def token(t, carry):
        h = h_ref[t]
        hh = h.astype(BF16).astype(F32)
        h16 = jnp.concatenate([hh, h - hh], axis=0).astype(BF16)
        zero = jnp.zeros_like(h16)
        lhs = jnp.concatenate([jnp.concatenate([h16, zero], axis=1),
                               jnp.concatenate([zero, h16], axis=1)], axis=0)
        for g in range(n_sel // GROUP):
            res = _dot_nt(lhs, _gather_tiles(tab_ref, row_ref, t, g * GROUP))
            za = jnp.sum((res[0:8] + res[8:16]) * m8, axis=0, keepdims=True)
            zb = jnp.sum((res[16:24] + res[24:32]) * m8, axis=0, keepdims=True)
            z_ref[pl.ds(t, 1), 2 * g * gw:(2 * g + 1) * gw] = za
            z_ref[pl.ds(t, 1), (2 * g + 1) * gw:(2 * g + 2) * gw] = zb
        return carry

    lax.fori_loop(0, h_ref.shape[0], token, 0, unroll=PEER_UNROLL)
    s2 = _dot_exact_rhs(z_ref[...], et_ref[...])
    s = jnp.where(sh_ref[...] == 0, s2[:, n_sel:], s2[:, :n_sel])
    act = s * (lax.erf(s * (2.0 ** -0.5)) + 1.0) * 0.5
    o_ref[...] = act * gate_ref[...]


def _peer_v_kernel(row_ref, coef_ref, sh_ref, e_ref, tab_ref, o_ref, la_ref, lb_ref, m_ref, *, n_sel):
    gw = (GROUP // 2) * SLOTS
    c = coef_ref[...]
    high = sh_ref[...] == 0
    c2 = jnp.concatenate([jnp.where(high, 0.0, c), jnp.where(high, c, 0.0)], axis=1)
    ca = c2.astype(BF16)
    cb = (c2 - ca.astype(F32)).astype(BF16)
    la_ref[...] = _dot(ca, e_ref[...])
    lb_ref[...] = _dot(cb, e_ref[...])
    m_ref[...] = _sublane_mask(2 * SUBLANES, n_sel * SLOTS)

    def token(t, carry):
        width = n_sel * SLOTS
        la = jnp.broadcast_to(la_ref[pl.ds(t, 1), :], (SUBLANES, width))
        lb = jnp.broadcast_to(lb_ref[pl.ds(t, 1), :], (SUBLANES, width))
        l16 = (jnp.concatenate([la, lb], axis=0) * m_ref[...]).astype(BF16)
        acc = None
        for g in range(n_sel // GROUP):
            lhs = jnp.concatenate([l16[:, 2 * g * gw:(2 * g + 1) * gw],
                                   l16[:, (2 * g + 1) * gw:(2 * g + 2) * gw]], axis=0)
            res = _dot(lhs, _gather_tiles(tab_ref, row_ref, t, g * GROUP))
            acc = res if acc is None else acc + res
        o_ref[t] = (acc[0:8, :LANES] + acc[8:16, :LANES]) + (acc[16:24, LANES:] + acc[24:32, LANES:])
        return carry

    lax.fori_loop(0, o_ref.shape[0], token, 0, unroll=PEER_UNROLL)


def _table_spec(tab):
    return pl.BlockSpec(tab.shape, lambda i: (0, 0), pipeline_mode=pl.Buffered(1))


def _smem_tile(tt, width):
    return pl.BlockSpec((tt, width), lambda i: (i, 0), memory_space=pltpu.SMEM)


def _peer_u(rows, shifts, h3, gates, tab):
    T, n_sel = rows.shape
    dd = h3.shape[1]
    assert dd == SUBLANES and n_sel % GROUP == 0
    tt = min(PEER_TOK, T)
    et = _expand_matrix(n_sel).T
    tile = pl.BlockSpec((tt, n_sel), lambda i: (i, 0))
    return pl.pallas_call(
        functools.partial(_peer_u_kernel, n_sel=n_sel),
        name="peer_u",
        grid=(T // tt,),
        in_specs=[_smem_tile(tt, n_sel),
                  pl.BlockSpec((tt, dd, LANES), lambda i: (i, 0, 0)),
                  tile, tile,
                  pl.BlockSpec(et.shape, lambda i: (0, 0)),
                  _table_spec(tab)],
        out_specs=tile,
        out_shape=jax.ShapeDtypeStruct((T, n_sel), F32),
        scratch_shapes=[pltpu.VMEM((tt, n_sel * SLOTS), F32)],
        compiler_params=_cparams(1),
    )(rows, h3, gates, shifts, et, tab)


def _peer_v(rows, shifts, coef, tab):
    T, n_sel = rows.shape
    dd = SUBLANES
    assert n_sel % GROUP == 0
    tt = min(PEER_TOK, T)
    e = _expand_matrix(n_sel)
    tile = pl.BlockSpec((tt, n_sel), lambda i: (i, 0))
    return pl.pallas_call(
        functools.partial(_peer_v_kernel, n_sel=n_sel),
        name="peer_v",
        grid=(T // tt,),
        in_specs=[_smem_tile(tt, n_sel), tile, tile, pl.BlockSpec(e.shape, lambda i: (0, 0)), _table_spec(tab)],
        out_specs=pl.BlockSpec((tt, dd, LANES), lambda i: (i, 0, 0)),
        out_shape=jax.ShapeDtypeStruct((T, dd, LANES), F32),
        scratch_shapes=[pltpu.VMEM((tt, n_sel * SLOTS), F32), pltpu.VMEM((tt, n_sel * SLOTS), F32),
                        pltpu.VMEM((2 * SUBLANES, n_sel * SLOTS), F32)],
        compiler_params=_cparams(1),
    )(rows, coef, shifts, e, tab)


def _ln2_kernel(x1_ref, y_ref, mod_ref, g_ref, b_ref, o_ref, *, alpha):
    m = mod_ref[0]
    o_ref[...] = _layer_norm(alpha * x1_ref[...] + m[5:6] * y_ref[...], g_ref[...], b_ref[...])


def _ln2(x1, y, mod3, g, b, S, alpha):
    T, D = x1.shape
    tt = min(512, S)
    tps = S // tt
    tile = pl.BlockSpec((tt, D), lambda i: (i, 0))
    vec = pl.BlockSpec((1, D), lambda i: (0, 0))
    return pl.pallas_call(
        functools.partial(_ln2_kernel, alpha=alpha),
        name="ln2",
        grid=(T // tt,),
        in_specs=[tile, tile, pl.BlockSpec((1, 6, D), lambda i: (i // tps, 0, 0)), vec, vec],
        out_specs=tile,
        out_shape=jax.ShapeDtypeStruct((T, D), F32),
        compiler_params=_cparams(1),
    )(x1, y, mod3, g.reshape(1, D), b.reshape(1, D))


def kernel(x, c, ada_w, ada_b, w_in, attn_sink, shift_mu, decay_w0, decay_w2, iclr_a0, iclr_a2, gate_g2, k_k, k_a,
           r_k, lnx_g, lnx_b, w_out, ln1_g, ln1_b, peer_wq, peer_subkeys, peer_u, peer_v, ln2_g, ln2_b):
    B, S, D = x.shape
    depth = ada_w.shape[0]
    alpha = (2.0 * depth) ** 0.25
    heads = attn_sink.shape[1]
    rheads = r_k.shape[1]
    rw = rheads * HEAD_DIM
    aw = heads * HEAD_DIM
    kw = (w_in.shape[2] - aw - shift_mu.shape[1]) // 2
    kv_heads = kw // HEAD_DIM
    pheads, _, nkeys, half = peer_subkeys.shape[1:]
    assert S % WINDOW == 0 and S % CHUNK == 0 and (B * S) % PEER_TOK == 0
    T = B * S
    xt = x.reshape(T, D)
    for l in range(depth):
        mod3 = _ada(c, ada_w[l], ada_b[l]).reshape(B, 6, D)
        q, k, v, pr = _proj(xt, mod3, w_in[l].astype(BF16), S, aw, kw)
        ya = _attn(attn_sink[l], q, k, v, B, S, heads, kv_heads)
        r, lw, kr, vr, kk, bb, g, bonus = _prep(pr, S, shift_mu[l], decay_w0[l], decay_w2[l], iclr_a0[l],
                                                iclr_a2[l], gate_g2[l], k_k[l], k_a[l], r_k[l].reshape(-1))
        ys = _scan(r, lw, kr, vr, kk, bb, S, rheads)
        x1, h2 = _post(ya, ys, bonus, g, xt, mod3, lnx_g[l], lnx_b[l], w_out[l].astype(BF16), ln1_g[l], ln1_b[l],
                       S, alpha)
        rows_t, sh_t, gate_t = _pkeys(h2, peer_wq[l].astype(BF16),
                                      peer_subkeys[l].reshape(pheads * 2, nkeys, half).astype(BF16),
                                      pheads, nkeys, half)
        rows, shifts, gates = rows_t.T, sh_t.T, gate_t.T
        coef = _peer_u(rows, shifts, h2.reshape(T, D // LANES, LANES), gates, _pack_table(peer_u[l]))
        y = _peer_v(rows, shifts, coef, _pack_table(peer_v[l])).reshape(T, D)
        xt = _ln2(x1, y, mod3, ln2_g[l], ln2_b[l], S, alpha)
    return xt.reshape(B, S, D)
```

```python
import functools
import math

import jax
import jax.numpy as jnp
from jax import lax
from jax.experimental import pallas as pl
from jax.experimental.pallas import tpu as pltpu

F32 = jnp.float32
BF16 = jnp.bfloat16

HEAD_DIM = 64
WINDOW = 128
TOPK = 16
LN_EPS = 1e-5
GN_EPS = 64e-5
LANES = 128
SUBLANES = 8
VMEM_LIMIT = 48 * 1024 * 1024
NEG = -1e30


def _cparams(n_axes, vmem=VMEM_LIMIT):
    return pltpu.CompilerParams(dimension_semantics=("arbitrary",) * n_axes, vmem_limit_bytes=vmem)


def _split3(x):
    hi = x.astype(BF16)
    r1 = x - hi.astype(F32)
    mid = r1.astype(BF16)
    lo = (r1 - mid.astype(F32)).astype(BF16)
    return hi, mid, lo


def _dot(a, b):
    return jnp.dot(a, b, preferred_element_type=F32)


def _dot_nt(a, b):
    return lax.dot_general(a, b, (((1,), (1,)), ((), ())), preferred_element_type=F32)


def _dot3(a, b):
    ah = a.astype(BF16)
    al = (a - ah.astype(F32)).astype(BF16)
    bh = b.astype(BF16)
    bl = (b - bh.astype(F32)).astype(BF16)
    return _dot(ah, bh) + _dot(ah, bl) + _dot(al, bh)


def _dot_exact_rhs(x, m_bf16):
    hi, mid, lo = _split3(x)
    return _dot(hi, m_bf16) + _dot(mid, m_bf16) + _dot(lo, m_bf16)


def _dot_exact_lhs(m_bf16, x):
    hi, mid, lo = _split3(x)
    return _dot(m_bf16, hi) + _dot(m_bf16, mid) + _dot(m_bf16, lo)


def _head_ones(width):
    r = lax.broadcasted_iota(jnp.int32, (width, width), 0) // HEAD_DIM
    c = lax.broadcasted_iota(jnp.int32, (width, width), 1) // HEAD_DIM
    return jnp.where(r == c, 1.0, 0.0).astype(BF16)


def _ada_kernel(c_ref, w_ref, b_ref, o_ref):
    c = c_ref[...]
    cs = c * jax.nn.sigmoid(c)
    o_ref[...] = _dot3(cs, w_ref[...]) + b_ref[...]


def _ada(c, w, b):
    B, D = c.shape
    N = w.shape[1]
    tn = 1536 if N % 1536 == 0 else N
    return pl.pallas_call(
        _ada_kernel,
        name="ada",
        grid=(N // tn,),
        in_specs=[pl.BlockSpec((B, D), lambda j: (0, 0)),
                  pl.BlockSpec((D, tn), lambda j: (0, j)),
                  pl.BlockSpec((1, tn), lambda j: (0, j))],
        out_specs=pl.BlockSpec((B, tn), lambda j: (0, j)),
        out_shape=jax.ShapeDtypeStruct((B, N), F32),
        compiler_params=_cparams(1),
    )(c, w, b.reshape(1, N))


def _proj_kernel(x_ref, mod_ref, w_ref, q_ref, k_ref, v_ref, p_ref, *, aw, kw):
    m = mod_ref[0]
    h = x_ref[...] * (1.0 + m[1:2]) + m[0:1]
    p = _dot(h.astype(BF16), w_ref[...])
    q_ref[...] = (p[:, :aw] * (HEAD_DIM ** -0.5)).astype(BF16)
    k_ref[...] = p[:, aw:aw + kw].astype(BF16)
    v_ref[...] = p[:, aw + kw:aw + 2 * kw].astype(BF16)
    p_ref[...] = p[:, aw + 2 * kw:]


def _proj(x2, mod3, w_bf, S, aw, kw):
    T, D = x2.shape
    N = w_bf.shape[1]
    rw = N - aw - 2 * kw
    tm = min(512, S)
    tps = S // tm
    return pl.pallas_call(
        functools.partial(_proj_kernel, aw=aw, kw=kw),
        name="proj",
        grid=(T // tm,),
        in_specs=[pl.BlockSpec((tm, D), lambda i: (i, 0)),
                  pl.BlockSpec((1, 6, D), lambda i: (i // tps, 0, 0)),
                  pl.BlockSpec((D, N), lambda i: (0, 0))],
        out_specs=[pl.BlockSpec((tm, aw), lambda i: (i, 0)),
                   pl.BlockSpec((tm, kw), lambda i: (i, 0)),
                   pl.BlockSpec((tm, kw), lambda i: (i, 0)),
                   pl.BlockSpec((tm, rw), lambda i: (i, 0))],
        out_shape=[jax.ShapeDtypeStruct((T, aw), BF16),
                   jax.ShapeDtypeStruct((T, kw), BF16),
                   jax.ShapeDtypeStruct((T, kw), BF16),
                   jax.ShapeDtypeStruct((T, rw), F32)],
        compiler_params=_cparams(1),
    )(x2, mod3, w_bf)


def _attn_kernel(sink_ref, q_ref, kp_ref, kc_ref, vp_ref, vc_ref, o_ref, *, heads, kv_heads):
    n = pl.program_id(1)
    groups = heads // kv_heads
    qi = lax.broadcasted_iota(jnp.int32, (WINDOW, 2 * WINDOW), 0)
    kj = lax.broadcasted_iota(jnp.int32, (WINDOW, 2 * WINDOW), 1)
    mask = (kj > qi) & (kj <= qi + WINDOW) & ((kj >= WINDOW) | (n > 0))
    for kh in range(kv_heads):
        ks = slice(kh * HEAD_DIM, (kh + 1) * HEAD_DIM)
        kwin = jnp.concatenate([kp_ref[:, ks], kc_ref[:, ks]], axis=0)
        vwin = jnp.concatenate([vp_ref[:, ks], vc_ref[:, ks]], axis=0)
        for g in range(groups):
            h = kh * groups + g
            hs = slice(h * HEAD_DIM, (h + 1) * HEAD_DIM)
            s = _dot_nt(q_ref[:, hs], kwin)
            s = jnp.where(mask, s, NEG)
            sk = sink_ref[h]
            m = jnp.maximum(jnp.max(s, axis=1, keepdims=True), sk)
            p = jnp.exp(s - m)
            den = jnp.sum(p, axis=1, keepdims=True) + jnp.exp(sk - m)
            o = _dot(p.astype(BF16), vwin) / den
            o_ref[:, hs] = o.astype(o_ref.dtype)


def _attn(sink, q, k, v, B, S, heads, kv_heads):
    T, aw = q.shape
    kw = k.shape[1]
    nb = S // WINDOW
    cur = lambda b, n: (b * nb + n, 0)
    prev = lambda b, n: (b * nb + jnp.maximum(n - 1, 0), 0)
    return pl.pallas_call(
        functools.partial(_attn_kernel, heads=heads, kv_heads=kv_heads),
        name="attn",
        grid=(B, nb),
        in_specs=[pl.BlockSpec(memory_space=pltpu.SMEM),
                  pl.BlockSpec((WINDOW, aw), cur),
                  pl.BlockSpec((WINDOW, kw), prev),
                  pl.BlockSpec((WINDOW, kw), cur),
                  pl.BlockSpec((WINDOW, kw), prev),
                  pl.BlockSpec((WINDOW, kw), cur)],
        out_specs=pl.BlockSpec((WINDOW, aw), cur),
        out_shape=jax.ShapeDtypeStruct((T, aw), BF16),
        compiler_params=_cparams(2),
    )(sink, q, k, k, v, v)


def _softplus(z):
    return jnp.maximum(z, 0.0) + jnp.log1p(jnp.exp(-jnp.abs(z)))


def _prep_kernel(p_ref, prev_ref, mu_ref, w0_ref, w2_ref, a0_ref, a2_ref, g2_ref, kk_ref, ka_ref, rk_ref,
                 r_o, lw_o, k_o, v_o, kk_o, b_o, g_o, bonus_o, *, rw, dl, al):
    i = pl.program_id(1)
    p = p_ref[...]
    last = jnp.where(i > 0, prev_ref[SUBLANES - 1:SUBLANES, :], 0.0)
    row = lax.broadcasted_iota(jnp.int32, p.shape, 0)
    p_prev = jnp.where(row == 0, last, pltpu.roll(p, 1, 0))
    p = p + (p_prev - p) * mu_ref[...]
    r = p[:, 0:rw]
    k = p[:, rw:2 * rw]
    v = p[:, 2 * rw:3 * rw]
    wd = p[:, 3 * rw:3 * rw + dl]
    ad = p[:, 3 * rw + dl:3 * rw + dl + al]
    gd = p[:, 3 * rw + dl + al:]
    ones = _head_ones(rw)
    w_log = -_softplus(-(w0_ref[...] + _dot3(jnp.tanh(wd), w2_ref[...]))) - 0.5
    a = jax.nn.sigmoid(a0_ref[...] + _dot3(ad, a2_ref[...]))
    g = _dot3(jax.nn.sigmoid(gd), g2_ref[...])
    kk = k * kk_ref[...]
    kk = kk * lax.rsqrt(jnp.maximum(_dot_exact_rhs(kk * kk, ones), 1e-24))
    k = k * (1.0 + (a - 1.0) * ka_ref[...])
    r_o[...] = r
    lw_o[...] = -jnp.exp(w_log)
    k_o[...] = k
    v_o[...] = v
    kk_o[...] = kk
    b_o[...] = kk * a
    g_o[...] = g
    bonus_o[...] = _dot_exact_rhs(r * k * rk_ref[...], ones) * v


def _prep(pr, S, mu, w0, w2, a0, a2, g2, k_k, k_a, r_k):
    T, cols = pr.shape
    rw = w0.shape[-1]
    dl, al = w2.shape[0], a2.shape[0]
    tt = min(256, S)
    tps = S // tt
    row = lambda a: a.reshape(1, -1)
    full = lambda a: pl.BlockSpec(a.shape, lambda b, i: (0,) * a.ndim)
    tile = lambda b, i: (b * tps + i, 0)
    prev = lambda b, i: (jnp.maximum((b * tps + i) * (tt // SUBLANES) - 1, 0), 0)
    args = [row(mu), row(w0), w2, row(a0), a2, g2, row(k_k), row(k_a), row(r_k)]
    out = jax.ShapeDtypeStruct((T, rw), F32)
    return pl.pallas_call(
        functools.partial(_prep_kernel, rw=rw, dl=dl, al=al),
        name="prep",
        grid=(T // S, tps),
        in_specs=[pl.BlockSpec((tt, cols), tile), pl.BlockSpec((SUBLANES, cols), prev)] + [full(a) for a in args],
        out_specs=[pl.BlockSpec((tt, rw), tile)] * 8,
        out_shape=[out] * 8,
        compiler_params=_cparams(2),
    )(pr, pr, *args)


CHUNK = 64


def _scan_kernel(r_ref, lw_ref, k_ref, v_ref, kk_ref, b_ref, y_ref, st_ref, *, heads):
    C = CHUNK

    @pl.when(pl.program_id(1) == 0)
    def _():
        st_ref[...] = jnp.zeros_like(st_ref)

    ri = lax.broadcasted_iota(jnp.int32, (C, C), 0)
    ci = lax.broadcasted_iota(jnp.int32, (C, C), 1)
    incl = ri >= ci
    strict = ri > ci
    lw = lw_ref[...]
    cum = _dot_exact_lhs(jnp.where(incl, 1.0, 0.0).astype(BF16), lw)
    pe = jnp.exp(cum)
    pinv = jnp.exp(-cum)
    rt = (r_ref[...] * pe).astype(BF16)
    nt = (-kk_ref[...] * jnp.exp(cum - lw)).astype(BF16)
    bt = (b_ref[...] * pinv).astype(BF16)
    kt = (k_ref[...] * pinv).astype(BF16)
    vb = v_ref[...].astype(BF16)
    hd = range(heads)
    hs = [slice(h * HEAD_DIM, (h + 1) * HEAD_DIM) for h in hd]
    nr = [jnp.concatenate([nt[:, s], rt[:, s]], axis=0) for s in hs]
    bh = [bt[:, s] for s in hs]
    kh = [kt[:, s] for s in hs]
    vh = [vb[:, s] for s in hs]
    ab = [_dot_nt(nr[h], bh[h]) for h in hd]
    ak = [_dot_nt(nr[h], kh[h]) for h in hd]
    s0 = [st_ref[h] for h in hd]
    ws = [_dot_nt(nr[h], s0[h].astype(BF16)) for h in hd]
    u = [ws[h][:C] + _dot(jnp.where(strict, ak[h][:C], 0.0).astype(BF16), vh[h]) for h in hd]
    ap = [jnp.where(strict, ab[h][:C], 0.0) for h in hd]
    steps = int(math.log2(C))
    for i in range(steps):
        apb = [a.astype(BF16) for a in ap]
        u = [u[h] + _dot(apb[h], u[h].astype(BF16)) for h in hd]
        if i + 1 < steps:
            ap = [_dot(a, a) for a in apb]
    ub = [x.astype(BF16) for x in u]
    y = [ws[h][C:] + _dot(jnp.where(incl, ab[h][C:], 0.0).astype(BF16), ub[h])
         + _dot(jnp.where(incl, ak[h][C:], 0.0).astype(BF16), vh[h]) for h in hd]
    for h in hd:
        y_ref[:, hs[h]] = y[h]
    uvt = [jnp.concatenate([u[h], v_ref[:, hs[h]]], axis=0).T.astype(BF16) for h in hd]
    sn = [s0[h] + _dot(uvt[h], jnp.concatenate([bh[h], kh[h]], axis=0)) for h in hd]
    for h in hd:
        st_ref[h] = sn[h] * pe[C - 1:C, hs[h]]


def _scan(r, lw, k, v, kk, b, S, heads):
    T, rw = r.shape
    nch = S // CHUNK
    tile = pl.BlockSpec((CHUNK, rw), lambda bb, c: (bb * nch + c, 0))
    return pl.pallas_call(
        functools.partial(_scan_kernel, heads=heads),
        name="scan",
        grid=(T // S, nch),
        in_specs=[tile] * 6,
        out_specs=tile,
        out_shape=jax.ShapeDtypeStruct((T, rw), F32),
        scratch_shapes=[pltpu.VMEM((heads, HEAD_DIM, HEAD_DIM), F32)],
        compiler_params=_cparams(2),
    )(r, lw, k, v, kk, b)


def _layer_norm(z, g, b):
    mu = jnp.mean(z, axis=-1, keepdims=True)
    d = z - mu
    var = jnp.mean(d * d, axis=-1, keepdims=True)
    return d * lax.rsqrt(var + LN_EPS) * g + b


def _post_kernel(ya_ref, ys_ref, bonus_ref, g_ref, x_ref, mod_ref, lg_ref, lb_ref, wo_ref, n1g_ref, n1b_ref,
                 x1_o, h2_o, *, alpha):
    m = mod_ref[0]
    ys = ys_ref[...]
    rw = ys.shape[1]
    aw = ya_ref.shape[1]
    ones = _head_ones(rw)
    mean = _dot_exact_rhs(ys, ones) * (1.0 / HEAD_DIM)
    d = ys - mean
    var = _dot_exact_rhs(d * d, ones) * (1.0 / HEAD_DIM)
    yr = (d * lax.rsqrt(var + GN_EPS) * lg_ref[...] + lb_ref[...] + bonus_ref[...]) * g_ref[...]
    y = _dot(ya_ref[...], wo_ref[0:aw, :]) + _dot(yr.astype(BF16), wo_ref[aw:aw + rw, :])
    x1 = _layer_norm(alpha * x_ref[...] + m[2:3] * y, n1g_ref[...], n1b_ref[...])
    x1_o[...] = x1
    h2_o[...] = x1 * (1.0 + m[4:5]) + m[3:4]


def _post(ya, ys, bonus, g, x2, mod3, lnx_g, lnx_b, wo_bf, ln1_g, ln1_b, S, alpha):
    T, D = x2.shape
    aw, rw = ya.shape[1], ys.shape[1]
    tt = min(256, S)
    tps = S // tt
    row = lambda a: a.reshape(1, -1)
    full = lambda a: pl.BlockSpec(a.shape, lambda i: (0,) * a.ndim)
    tile = lambda w: pl.BlockSpec((tt, w), lambda i: (i, 0))
    consts = [row(lnx_g), row(lnx_b), wo_bf, row(ln1_g), row(ln1_b)]
    return pl.pallas_call(
        functools.partial(_post_kernel, alpha=alpha),
        name="post",
        grid=(T // tt,),
        in_specs=[tile(aw), tile(rw), tile(rw), tile(rw), tile(D),
                  pl.BlockSpec((1, 6, D), lambda i: (i // tps, 0, 0))] + [full(a) for a in consts],
        out_specs=[tile(D), tile(D)],
        out_shape=[jax.ShapeDtypeStruct((T, D), F32)] * 2,
        compiler_params=_cparams(1),
    )(ya, ys, bonus, g, x2, mod3, *consts)


def _top_rows(s, count, payload=None):
    nrow = -lax.broadcasted_iota(jnp.int32, s.shape, 0).astype(F32)
    vals, pays = [], []
    for _ in range(count):
        m = jnp.max(s, axis=0, keepdims=True)
        first = jnp.max(jnp.where(s == m, nrow, -jnp.inf), axis=0, keepdims=True)
        hit = nrow == first
        vals.append(m)
        pays.append(-first if payload is None else jnp.sum(jnp.where(hit, payload, 0.0), axis=0, keepdims=True))
        s = jnp.where(hit, -jnp.inf, s)
    return vals, pays


def _pkeys_kernel(h_ref, wq_ref, sk_ref, row_o, sh_o, gate_o, *, heads, nkeys, half):
    q = _dot(h_ref[...].astype(BF16), wq_ref[...])
    L = q.shape[0]
    j8 = lax.broadcasted_iota(jnp.int32, (SUBLANES, L), 0)
    ninf = -jnp.inf
    fvs, fes = [], []
    for h in range(heads):
        sv, si = [], []
        for c in range(2):
            qc = q[:, (2 * h + c) * half:(2 * h + c + 1) * half].astype(BF16)
            v_, i_ = _top_rows(_dot_nt(sk_ref[2 * h + c], qc), TOPK)
            sv.append(v_)
            si.append(i_)
        sv1 = jnp.concatenate(sv[1], axis=0)
        si1 = jnp.concatenate(si[1], axis=0)
        cv = [sv[0][0] + sv1]
        ce = [si[0][0] * nkeys + si1]
        for i in range(1, SUBLANES):
            cv.append(jnp.where(j8 < TOPK // (i + 1), sv[0][i] + sv1[:SUBLANES], ninf))
            ce.append(si[0][i] * nkeys + si1[:SUBLANES])
        cv.append(jnp.concatenate(sv[0][SUBLANES:], axis=0) + sv1[0:1])
        ce.append(jnp.concatenate(si[0][SUBLANES:], axis=0) * nkeys + si1[0:1])
        fv, fe = _top_rows(jnp.concatenate(cv, axis=0), TOPK, jnp.concatenate(ce, axis=0))
        fvs.append(jnp.concatenate(fv, axis=0))
        fes.append(jnp.concatenate(fe, axis=0))
    fv = jnp.concatenate(fvs, axis=0).T
    fe = jnp.concatenate(fes, axis=0).T
    n_out = heads * TOPK
    same = jnp.where(lax.broadcasted_iota(jnp.int32, (n_out, n_out), 0) // TOPK
                     == lax.broadcasted_iota(jnp.int32, (n_out, n_out), 1) // TOPK,
                     1.0, 0.0).astype(BF16)
    top = _dot_exact_rhs(jnp.where(lax.broadcasted_iota(jnp.int32, fv.shape, 1) % TOPK == 0, fv, 0.0), same)
    e = jnp.exp(fv - top)
    gate_o[...] = e / _dot_exact_rhs(e, same)
    pairs = float(nkeys * nkeys // 2)
    high = fe < pairs
    row_o[...] = (jnp.where(high, fe, fe - pairs) * SUBLANES).astype(jnp.int32)
    sh_o[...] = jnp.where(high, 0, 16).astype(jnp.int32)


def _pkeys(h2, wq_bf, sk_bf, heads, nkeys, half):
    T, D = h2.shape
    tt = min(256, T)
    n_out = heads * TOPK
    out_spec = pl.BlockSpec((tt, n_out), lambda i: (i, 0))
    return pl.pallas_call(
        functools.partial(_pkeys_kernel, heads=heads, nkeys=nkeys, half=half),
        name="pkeys",
        grid=(T // tt,),
        in_specs=[pl.BlockSpec((tt, D), lambda i: (i, 0)),
                  pl.BlockSpec(wq_bf.shape, lambda i: (0, 0)),
                  pl.BlockSpec(sk_bf.shape, lambda i: (0, 0, 0))],
        out_specs=[out_spec] * 3,
        out_shape=[jax.ShapeDtypeStruct((T, n_out), jnp.int32),
                   jax.ShapeDtypeStruct((T, n_out), jnp.int32),
                   jax.ShapeDtypeStruct((T, n_out), F32)],
        compiler_params=_cparams(1),
    )(h2, wq_bf, sk_bf)


PEER_TOK = 128
PEER_UNROLL = 8


def _pack_table(tab):
    n, d = tab.shape
    assert d == SUBLANES * LANES
    bits = lax.bitcast_convert_type(tab.astype(BF16), jnp.uint16).astype(jnp.uint32)
    packed = (bits[:n // 2] << 16) | bits[n // 2:]
    return packed.reshape(n // 2 * SUBLANES, LANES)


GROUP = 32
SLOTS = 2 * SUBLANES


def _expand_matrix(n_sel):
    r = lax.broadcasted_iota(jnp.int32, (2 * n_sel, n_sel * SLOTS), 0)
    c = lax.broadcasted_iota(jnp.int32, (2 * n_sel, n_sel * SLOTS), 1)
    return jnp.where((r % n_sel == c // SLOTS) & (r // n_sel == c % 2), 1.0, 0.0).astype(BF16)


def _sublane_mask(rows, cols):
    r = lax.broadcasted_iota(jnp.int32, (rows, cols), 0)
    c = lax.broadcasted_iota(jnp.int32, (rows, cols), 1)
    return jnp.where(r % SUBLANES == (c % SLOTS) // 2, 1.0, 0.0).astype(F32)


def _gather_tiles(tab_ref, row_ref, t, first):
    half = GROUP // 2
    tile = lambda k: pltpu.bitcast(
        tab_ref[pl.ds(pl.multiple_of(row_ref[t, k], SUBLANES), SUBLANES), :], BF16)
    a = jnp.concatenate([tile(first + i) for i in range(half)], axis=0)
    b = jnp.concatenate([tile(first + half + i) for i in range(half)], axis=0)
    return jnp.concatenate([a, b], axis=1)


def _peer_u_kernel(row_ref, h_ref, gate_ref, sh_ref, et_ref, tab_ref, o_ref, z_ref, *, n_sel):
    gw = (GROUP // 2) * SLOTS
    m8 = _sublane_mask(SUBLANES, gw)

    def token(t, carry):
        h = h_ref[t]
        hh = h.astype(BF16).astype(F32)
        h16 = jnp.concatenate([hh, h - hh], axis=0).astype(BF16)
        zero = jnp.zeros_like(h16)
        lhs = jnp.concatenate([jnp.concatenate([h16, zero], axis=1),
                               jnp.concatenate([zero, h16], axis=1)], axis=0)
        for g in range(n_sel // GROUP):
            res = _dot_nt(lhs, _gather_tiles(tab_ref, row_ref, t, g * GROUP))
            za = jnp.sum((res[0:8] + res[8:16]) * m8, axis=0, keepdims=True)
            zb = jnp.sum((res[16:24] + res[24:32]) * m8, axis=0, keepdims=True)
            z_ref[pl.ds(t, 1), 2 * g * gw:(2 * g + 1) * gw] = za
            z_ref[pl.ds(t, 1), (2 * g + 1) * gw:(2 * g + 2) * gw] = zb
        return carry

    lax.fori_loop(0, h_ref.shape[0], token, 0, unroll=PEER_UNROLL)
    s2 = _dot_exact_rhs(z_ref[...], et_ref[...])
    s = jnp.where(sh_ref[...] == 0, s2[:, n_sel:], s2[:, :n_sel])
    act = s * (lax.erf(s * (2.0 ** -0.5)) + 1.0) * 0.5
    o_ref[...] = act * gate_ref[...]


def _peer_v_kernel(row_ref, coef_ref, sh_ref, e_ref, tab_ref, o_ref, la_ref, lb_ref, m_ref, *, n_sel):
    gw = (GROUP // 2) * SLOTS
    c = coef_ref[...]
    high = sh_ref[...] == 0
    c2 = jnp.concatenate([jnp.where(high, 0.0, c), jnp.where(high, c, 0.0)], axis=1)
    ca = c2.astype(BF16)
    cb = (c2 - ca.astype(F32)).astype(BF16)
    la_ref[...] = _dot(ca, e_ref[...])
    lb_ref[...] = _dot(cb, e_ref[...])
    m_ref[...] = _sublane_mask(2 * SUBLANES, n_sel * SLOTS)

    def token(t, carry):
        width = n_sel * SLOTS
        la = jnp.broadcast_to(la_ref[pl.ds(t, 1), :], (SUBLANES, width))
        lb = jnp.broadcast_to(lb_ref[pl.ds(t, 1), :], (SUBLANES, width))
        l16 = (jnp.concatenate([la, lb], axis=0) * m_ref[...]).astype(BF16)
        acc = None
        for g in range(n_sel // GROUP):
            lhs = jnp.concatenate([l16[:, 2 * g * gw:(2 * g + 1) * gw],
                                   l16[:, (2 * g + 1) * gw:(2 * g + 2) * gw]], axis=0)
            res = _dot(lhs, _gather_tiles(tab_ref, row_ref, t, g * GROUP))
            acc = res if acc is None else acc + res
        o_ref[t] = (acc[0:8, :LANES] + acc[8:16, :LANES]) + (acc[16:24, LANES:] + acc[24:32, LANES:])
        return carry

    lax.fori_loop(0, o_ref.shape[0], token, 0, unroll=PEER_UNROLL)


def _table_spec(tab):
    return pl.BlockSpec(tab.shape, lambda i: (0, 0), pipeline_mode=pl.Buffered(1))


def _smem_tile(tt, width):
    return pl.BlockSpec((tt, width), lambda i: (i, 0), memory_space=pltpu.SMEM)


def _peer_u(rows, shifts, h3, gates, tab):
    T, n_sel = rows.shape
    dd = h3.shape[1]
    assert dd == SUBLANES and n_sel % GROUP == 0
    tt = min(PEER_TOK, T)
    et = _expand_matrix(n_sel).T
    tile = pl.BlockSpec((tt, n_sel), lambda i: (i, 0))
    return pl.pallas_call(
        functools.partial(_peer_u_kernel, n_sel=n_sel),
        name="peer_u",
        grid=(T // tt,),
        in_specs=[_smem_tile(tt, n_sel),
                  pl.BlockSpec((tt, dd, LANES), lambda i: (i, 0, 0)),
                  tile, tile,
                  pl.BlockSpec(et.shape, lambda i: (0, 0)),
                  _table_spec(tab)],
        out_specs=tile,
        out_shape=jax.ShapeDtypeStruct((T, n_sel), F32),
        scratch_shapes=[pltpu.VMEM((tt, n_sel * SLOTS), F32)],
        compiler_params=_cparams(1),
    )(rows, h3, gates, shifts, et, tab)


def _peer_v(rows, shifts, coef, tab):
    T, n_sel = rows.shape
    dd = SUBLANES
    assert n_sel % GROUP == 0
    tt = min(PEER_TOK, T)
    e = _expand_matrix(n_sel)
    tile = pl.BlockSpec((tt, n_sel), lambda i: (i, 0))
    return pl.pallas_call(
        functools.partial(_peer_v_kernel, n_sel=n_sel),
        name="peer_v",
        grid=(T // tt,),
        in_specs=[_smem_tile(tt, n_sel), tile, tile, pl.BlockSpec(e.shape, lambda i: (0, 0)), _table_spec(tab)],
        out_specs=pl.BlockSpec((tt, dd, LANES), lambda i: (i, 0, 0)),
        out_shape=jax.ShapeDtypeStruct((T, dd, LANES), F32),
        scratch_shapes=[pltpu.VMEM((tt, n_sel * SLOTS), F32), pltpu.VMEM((tt, n_sel * SLOTS), F32),
                        pltpu.VMEM((2 * SUBLANES, n_sel * SLOTS), F32)],
        compiler_params=_cparams(1),
    )(rows, coef, shifts, e, tab)


def _ln2_kernel(x1_ref, y_ref, mod_ref, g_ref, b_ref, o_ref, *, alpha):
    m = mod_ref[0]
    o_ref[...] = _layer_norm(alpha * x1_ref[...] + m[5:6] * y_ref[...], g_ref[...], b_ref[...])


def _ln2(x1, y, mod3, g, b, S, alpha):
    T, D = x1.shape
    tt = min(512, S)
    tps = S // tt
    tile = pl.BlockSpec((tt, D), lambda i: (i, 0))
    vec = pl.BlockSpec((1, D), lambda i: (0, 0))
    return pl.pallas_call(
        functools.partial(_ln2_kernel, alpha=alpha),
        name="ln2",
        grid=(T // tt,),
        in_specs=[tile, tile, pl.BlockSpec((1, 6, D), lambda i: (i // tps, 0, 0)), vec, vec],
        out_specs=tile,
        out_shape=jax.ShapeDtypeStruct((T, D), F32),
        compiler_params=_cparams(1),
    )(x1, y, mod3, g.reshape(1, D), b.reshape(1, D))


def kernel(x, c, ada_w, ada_b, w_in, attn_sink, shift_mu, decay_w0, decay_w2, iclr_a0, iclr_a2, gate_g2, k_k, k_a,
           r_k, lnx_g, lnx_b, w_out, ln1_g, ln1_b, peer_wq, peer_subkeys, peer_u, peer_v, ln2_g, ln2_b):
    B, S, D = x.shape
    depth = ada_w.shape[0]
    alpha = (2.0 * depth) ** 0.25
    heads = attn_sink.shape[1]
    rheads = r_k.shape[1]
    rw = rheads * HEAD_DIM
    aw = heads * HEAD_DIM
    kw = (w_in.shape[2] - aw - shift_mu.shape[1]) // 2
    kv_heads = kw // HEAD_DIM
    pheads, _, nkeys, half = peer_subkeys.shape[1:]
    assert S % WINDOW == 0 and S % CHUNK == 0 and (B * S) % PEER_TOK == 0
    T = B * S
    xt = x.reshape(T, D)
    for l in range(depth):
        mod3 = _ada(c, ada_w[l], ada_b[l]).reshape(B, 6, D)
        q, k, v, pr = _proj(xt, mod3, w_in[l].astype(BF16), S, aw, kw)
        ya = _attn(attn_sink[l], q, k, v, B, S, heads, kv_heads)
        r, lw, kr, vr, kk, bb, g, bonus = _prep(pr, S, shift_mu[l], decay_w0[l], decay_w2[l], iclr_a0[l],
                                                iclr_a2[l], gate_g2[l], k_k[l], k_a[l], r_k[l].reshape(-1))
        ys = _scan(r, lw, kr, vr, kk, bb, S, rheads)
        x1, h2 = _post(ya, ys, bonus, g, xt, mod3, lnx_g[l], lnx_b[l], w_out[l].astype(BF16), ln1_g[l], ln1_b[l],
                       S, alpha)
        rows, shifts, gates = _pkeys(h2, peer_wq[l].astype(BF16),
                                     peer_subkeys[l].reshape(pheads * 2, nkeys, half).astype(BF16),
                                     pheads, nkeys, half)
        coef = _peer_u(rows, shifts, h2.reshape(T, D // LANES, LANES), gates, _pack_table(peer_u[l]))
        y = _peer_v(rows, shifts, coef, _pack_table(peer_v[l])).reshape(T, D)
        xt = _ln2(x1, y, mod3, ln2_g[l], ln2_b[l], S, alpha)
    return xt.reshape(B, S, D)
```

```python
import functools
import math

import jax
import jax.numpy as jnp
from jax import lax
from jax.experimental import pallas as pl
from jax.experimental.pallas import tpu as pltpu

F32 = jnp.float32
BF16 = jnp.bfloat16

HEAD_DIM = 64
WINDOW = 128
TOPK = 16
LN_EPS = 1e-5
GN_EPS = 64e-5
LANES = 128
SUBLANES = 8
VMEM_LIMIT = 48 * 1024 * 1024
NEG = -1e30


def _cparams(n_axes, vmem=VMEM_LIMIT):
    return pltpu.CompilerParams(dimension_semantics=("arbitrary",) * n_axes, vmem_limit_bytes=vmem)


def _split3(x):
    hi = x.astype(BF16)
    r1 = x - hi.astype(F32)
    mid = r1.astype(BF16)
    lo = (r1 - mid.astype(F32)).astype(BF16)
    return hi, mid, lo


def _dot(a, b):
    return jnp.dot(a, b, preferred_element_type=F32)


def _dot_nt(a, b):
    return lax.dot_general(a, b, (((1,), (1,)), ((), ())), preferred_element_type=F32)


def _dot3(a, b):
    ah = a.astype(BF16)
    al = (a - ah.astype(F32)).astype(BF16)
    bh = b.astype(BF16)
    bl = (b - bh.astype(F32)).astype(BF16)
    return _dot(ah, bh) + _dot(ah, bl) + _dot(al, bh)


def _dot_exact_rhs(x, m_bf16):
    hi, mid, lo = _split3(x)
    return _dot(hi, m_bf16) + _dot(mid, m_bf16) + _dot(lo, m_bf16)


def _dot_exact_lhs(m_bf16, x):
    hi, mid, lo = _split3(x)
    return _dot(m_bf16, hi) + _dot(m_bf16, mid) + _dot(m_bf16, lo)


def _head_ones(width):
    r = lax.broadcasted_iota(jnp.int32, (width, width), 0) // HEAD_DIM
    c = lax.broadcasted_iota(jnp.int32, (width, width), 1) // HEAD_DIM
    return jnp.where(r == c, 1.0, 0.0).astype(BF16)


def _ada_kernel(c_ref, w_ref, b_ref, o_ref):
    c = c_ref[...]
    cs = c * jax.nn.sigmoid(c)
    o_ref[...] = _dot3(cs, w_ref[...]) + b_ref[...]


def _ada(c, w, b):
    B, D = c.shape
    N = w.shape[1]
    tn = 1536 if N % 1536 == 0 else N
    return pl.pallas_call(
        _ada_kernel,
        name="ada",
        grid=(N // tn,),
        in_specs=[pl.BlockSpec((B, D), lambda j: (0, 0)),
                  pl.BlockSpec((D, tn), lambda j: (0, j)),
                  pl.BlockSpec((1, tn), lambda j: (0, j))],
        out_specs=pl.BlockSpec((B, tn), lambda j: (0, j)),
        out_shape=jax.ShapeDtypeStruct((B, N), F32),
        compiler_params=_cparams(1),
    )(c, w, b.reshape(1, N))


def _proj_kernel(x_ref, mod_ref, w_ref, q_ref, k_ref, v_ref, p_ref, *, aw, kw):
    m = mod_ref[0]
    h = x_ref[...] * (1.0 + m[1:2]) + m[0:1]
    p = _dot(h.astype(BF16), w_ref[...])
    q_ref[...] = (p[:, :aw] * (HEAD_DIM ** -0.5)).astype(BF16)
    k_ref[...] = p[:, aw:aw + kw].astype(BF16)
    v_ref[...] = p[:, aw + kw:aw + 2 * kw].astype(BF16)
    p_ref[...] = p[:, aw + 2 * kw:]


def _proj(x2, mod3, w_bf, S, aw, kw):
    T, D = x2.shape
    N = w_bf.shape[1]
    rw = N - aw - 2 * kw
    tm = min(512, S)
    tps = S // tm
    return pl.pallas_call(
        functools.partial(_proj_kernel, aw=aw, kw=kw),
        name="proj",
        grid=(T // tm,),
        in_specs=[pl.BlockSpec((tm, D), lambda i: (i, 0)),
                  pl.BlockSpec((1, 6, D), lambda i: (i // tps, 0, 0)),
                  pl.BlockSpec((D, N), lambda i: (0, 0))],
        out_specs=[pl.BlockSpec((tm, aw), lambda i: (i, 0)),
                   pl.BlockSpec((tm, kw), lambda i: (i, 0)),
                   pl.BlockSpec((tm, kw), lambda i: (i, 0)),
                   pl.BlockSpec((tm, rw), lambda i: (i, 0))],
        out_shape=[jax.ShapeDtypeStruct((T, aw), BF16),
                   jax.ShapeDtypeStruct((T, kw), BF16),
                   jax.ShapeDtypeStruct((T, kw), BF16),
                   jax.ShapeDtypeStruct((T, rw), F32)],
        compiler_params=_cparams(1),
    )(x2, mod3, w_bf)


def _attn_kernel(sink_ref, q_ref, kp_ref, kc_ref, vp_ref, vc_ref, o_ref, *, heads, kv_heads):
    n = pl.program_id(1)
    groups = heads // kv_heads
    qi = lax.broadcasted_iota(jnp.int32, (WINDOW, 2 * WINDOW), 0)
    kj = lax.broadcasted_iota(jnp.int32, (WINDOW, 2 * WINDOW), 1)
    mask = (kj > qi) & (kj <= qi + WINDOW) & ((kj >= WINDOW) | (n > 0))
    mask = jnp.concatenate([mask] * groups, axis=0)
    grp = lax.broadcasted_iota(jnp.int32, (groups * WINDOW, 1), 0) // WINDOW
    kvs = range(kv_heads)
    ks = [slice(kh * HEAD_DIM, (kh + 1) * HEAD_DIM) for kh in kvs]
    kwin = [jnp.concatenate([kp_ref[:, s], kc_ref[:, s]], axis=0) for s in ks]
    vwin = [jnp.concatenate([vp_ref[:, s], vc_ref[:, s]], axis=0) for s in ks]
    hs = lambda kh, g: slice((kh * groups + g) * HEAD_DIM, (kh * groups + g + 1) * HEAD_DIM)
    qs = [jnp.concatenate([q_ref[:, hs(kh, g)] for g in range(groups)], axis=0) for kh in kvs]
    sk = []
    for kh in kvs:
        col = jnp.full((groups * WINDOW, 1), sink_ref[kh * groups], F32)
        for g in range(1, groups):
            col = jnp.where(grp == g, sink_ref[kh * groups + g], col)
        sk.append(col)
    s = [jnp.where(mask, _dot_nt(qs[kh], kwin[kh]), NEG) for kh in kvs]
    m = [jnp.maximum(jnp.max(s[kh], axis=1, keepdims=True), sk[kh]) for kh in kvs]
    p = [jnp.exp(s[kh] - m[kh]) for kh in kvs]
    den = [jnp.sum(p[kh], axis=1, keepdims=True) + jnp.exp(sk[kh] - m[kh]) for kh in kvs]
    o = [_dot(p[kh].astype(BF16), vwin[kh]) / den[kh] for kh in kvs]
    for kh in kvs:
        for g in range(groups):
            o_ref[:, hs(kh, g)] = o[kh][g * WINDOW:(g + 1) * WINDOW].astype(o_ref.dtype)


def _attn(sink, q, k, v, B, S, heads, kv_heads):
    T, aw = q.shape
    kw = k.shape[1]
    nb = S // WINDOW
    cur = lambda b, n: (b * nb + n, 0)
    prev = lambda b, n: (b * nb + jnp.maximum(n - 1, 0), 0)
    return pl.pallas_call(
        functools.partial(_attn_kernel, heads=heads, kv_heads=kv_heads),
        name="attn",
        grid=(B, nb),
        in_specs=[pl.BlockSpec(memory_space=pltpu.SMEM),
                  pl.BlockSpec((WINDOW, aw), cur),
                  pl.BlockSpec((WINDOW, kw), prev),
                  pl.BlockSpec((WINDOW, kw), cur),
                  pl.BlockSpec((WINDOW, kw), prev),
                  pl.BlockSpec((WINDOW, kw), cur)],
        out_specs=pl.BlockSpec((WINDOW, aw), cur),
        out_shape=jax.ShapeDtypeStruct((T, aw), BF16),
        compiler_params=_cparams(2),
    )(sink, q, k, k, v, v)


def _softplus(z):
    return jnp.maximum(z, 0.0) + jnp.log1p(jnp.exp(-jnp.abs(z)))


def _prep_kernel(p_ref, prev_ref, mu_ref, w0_ref, w2_ref, a0_ref, a2_ref, g2_ref, kk_ref, ka_ref, rk_ref,
                 r_o, lw_o, k_o, v_o, kk_o, b_o, g_o, bonus_o, *, rw, dl, al):
    i = pl.program_id(1)
    p = p_ref[...]
    last = jnp.where(i > 0, prev_ref[SUBLANES - 1:SUBLANES, :], 0.0)
    row = lax.broadcasted_iota(jnp.int32, p.shape, 0)
    p_prev = jnp.where(row == 0, last, pltpu.roll(p, 1, 0))
    p = p + (p_prev - p) * mu_ref[...]
    r = p[:, 0:rw]
    k = p[:, rw:2 * rw]
    v = p[:, 2 * rw:3 * rw]
    wd = p[:, 3 * rw:3 * rw + dl]
    ad = p[:, 3 * rw + dl:3 * rw + dl + al]
    gd = p[:, 3 * rw + dl + al:]
    ones = _head_ones(rw)
    w_log = -_softplus(-(w0_ref[...] + _dot3(jnp.tanh(wd), w2_ref[...]))) - 0.5
    a = jax.nn.sigmoid(a0_ref[...] + _dot3(ad, a2_ref[...]))
    g = _dot3(jax.nn.sigmoid(gd), g2_ref[...])
    kk = k * kk_ref[...]
    kk = kk * lax.rsqrt(jnp.maximum(_dot_exact_rhs(kk * kk, ones), 1e-24))
    k = k * (1.0 + (a - 1.0) * ka_ref[...])
    r_o[...] = r
    lw_o[...] = -jnp.exp(w_log)
    k_o[...] = k
    v_o[...] = v
    kk_o[...] = kk
    b_o[...] = kk * a
    g_o[...] = g
    bonus_o[...] = _dot_exact_rhs(r * k * rk_ref[...], ones) * v


def _prep(pr, S, mu, w0, w2, a0, a2, g2, k_k, k_a, r_k):
    T, cols = pr.shape
    rw = w0.shape[-1]
    dl, al = w2.shape[0], a2.shape[0]
    tt = min(256, S)
    tps = S // tt
    row = lambda a: a.reshape(1, -1)
    full = lambda a: pl.BlockSpec(a.shape, lambda b, i: (0,) * a.ndim)
    tile = lambda b, i: (b * tps + i, 0)
    prev = lambda b, i: (jnp.maximum((b * tps + i) * (tt // SUBLANES) - 1, 0), 0)
    args = [row(mu), row(w0), w2, row(a0), a2, g2, row(k_k), row(k_a), row(r_k)]
    out = jax.ShapeDtypeStruct((T, rw), F32)
    return pl.pallas_call(
        functools.partial(_prep_kernel, rw=rw, dl=dl, al=al),
        name="prep",
        grid=(T // S, tps),
        in_specs=[pl.BlockSpec((tt, cols), tile), pl.BlockSpec((SUBLANES, cols), prev)] + [full(a) for a in args],
        out_specs=[pl.BlockSpec((tt, rw), tile)] * 8,
        out_shape=[out] * 8,
        compiler_params=_cparams(2),
    )(pr, pr, *args)


CHUNK = 64


def _scan_kernel(r_ref, lw_ref, k_ref, v_ref, kk_ref, b_ref, y_ref, st_ref, *, heads):
    C = CHUNK

    @pl.when(pl.program_id(1) == 0)
    def _():
        st_ref[...] = jnp.zeros_like(st_ref)

    ri = lax.broadcasted_iota(jnp.int32, (C, C), 0)
    ci = lax.broadcasted_iota(jnp.int32, (C, C), 1)
    incl = ri >= ci
    strict = ri > ci
    lw = lw_ref[...]
    cum = _dot_exact_lhs(jnp.where(incl, 1.0, 0.0).astype(BF16), lw)
    pe = jnp.exp(cum)
    pinv = jnp.exp(-cum)
    rt = (r_ref[...] * pe).astype(BF16)
    nt = (-kk_ref[...] * jnp.exp(cum - lw)).astype(BF16)
    bt = (b_ref[...] * pinv).astype(BF16)
    kt = (k_ref[...] * pinv).astype(BF16)
    vb = v_ref[...].astype(BF16)
    hd = range(heads)
    hs = [slice(h * HEAD_DIM, (h + 1) * HEAD_DIM) for h in hd]
    nr = [jnp.concatenate([nt[:, s], rt[:, s]], axis=0) for s in hs]
    bh = [bt[:, s] for s in hs]
    kh = [kt[:, s] for s in hs]
    vh = [vb[:, s] for s in hs]
    ab = [_dot_nt(nr[h], bh[h]) for h in hd]
    ak = [_dot_nt(nr[h], kh[h]) for h in hd]
    s0 = [st_ref[h] for h in hd]
    ws = [_dot_nt(nr[h], s0[h].astype(BF16)) for h in hd]
    u = [ws[h][:C] + _dot(jnp.where(strict, ak[h][:C], 0.0).astype(BF16), vh[h]) for h in hd]
    ap = [jnp.where(strict, ab[h][:C], 0.0) for h in hd]
    steps = int(math.log2(C))
    for i in range(steps):
        apb = [a.astype(BF16) for a in ap]
        u = [u[h] + _dot(apb[h], u[h].astype(BF16)) for h in hd]
        if i + 1 < steps:
            ap = [_dot(a, a) for a in apb]
    ub = [x.astype(BF16) for x in u]
    y = [ws[h][C:] + _dot(jnp.where(incl, ab[h][C:], 0.0).astype(BF16), ub[h])
         + _dot(jnp.where(incl, ak[h][C:], 0.0).astype(BF16), vh[h]) for h in hd]
    for h in hd:
        y_ref[:, hs[h]] = y[h]
    uvt = [jnp.concatenate([u[h], v_ref[:, hs[h]]], axis=0).T.astype(BF16) for h in hd]
    sn = [s0[h] + _dot(uvt[h], jnp.concatenate([bh[h], kh[h]], axis=0)) for h in hd]
    for h in hd:
        st_ref[h] = sn[h] * pe[C - 1:C, hs[h]]


def _scan(r, lw, k, v, kk, b, S, heads):
    T, rw = r.shape
    nch = S // CHUNK
    tile = pl.BlockSpec((CHUNK, rw), lambda bb, c: (bb * nch + c, 0))
    return pl.pallas_call(
        functools.partial(_scan_kernel, heads=heads),
        name="scan",
        grid=(T // S, nch),
        in_specs=[tile] * 6,
        out_specs=tile,
        out_shape=jax.ShapeDtypeStruct((T, rw), F32),
        scratch_shapes=[pltpu.VMEM((heads, HEAD_DIM, HEAD_DIM), F32)],
        compiler_params=_cparams(2),
    )(r, lw, k, v, kk, b)


def _layer_norm(z, g, b):
    mu = jnp.mean(z, axis=-1, keepdims=True)
    d = z - mu
    var = jnp.mean(d * d, axis=-1, keepdims=True)
    return d * lax.rsqrt(var + LN_EPS) * g + b


def _store_tiles(ref3, x):
    for s in range(ref3.shape[1]):
        ref3[:, s, :] = x[:, s * LANES:(s + 1) * LANES]


def _load_tiles(ref3):
    return jnp.concatenate([ref3[:, s, :] for s in range(ref3.shape[1])], axis=1)


def _post_kernel(ya_ref, ys_ref, bonus_ref, g_ref, x_ref, mod_ref, lg_ref, lb_ref, wo_ref, n1g_ref, n1b_ref,
                 x1_o, h2_o, *, alpha):
    m = mod_ref[0]
    ys = ys_ref[...]
    rw = ys.shape[1]
    aw = ya_ref.shape[1]
    ones = _head_ones(rw)
    mean = _dot_exact_rhs(ys, ones) * (1.0 / HEAD_DIM)
    d = ys - mean
    var = _dot_exact_rhs(d * d, ones) * (1.0 / HEAD_DIM)
    yr = (d * lax.rsqrt(var + GN_EPS) * lg_ref[...] + lb_ref[...] + bonus_ref[...]) * g_ref[...]
    y = _dot(ya_ref[...], wo_ref[0:aw, :]) + _dot(yr.astype(BF16), wo_ref[aw:aw + rw, :])
    x1 = _layer_norm(alpha * x_ref[...] + m[2:3] * y, n1g_ref[...], n1b_ref[...])
    x1_o[...] = x1
    _store_tiles(h2_o, x1 * (1.0 + m[4:5]) + m[3:4])


def _post(ya, ys, bonus, g, x2, mod3, lnx_g, lnx_b, wo_bf, ln1_g, ln1_b, S, alpha):
    T, D = x2.shape
    aw, rw = ya.shape[1], ys.shape[1]
    tt = min(256, S)
    tps = S // tt
    row = lambda a: a.reshape(1, -1)
    full = lambda a: pl.BlockSpec(a.shape, lambda i: (0,) * a.ndim)
    tile = lambda w: pl.BlockSpec((tt, w), lambda i: (i, 0))
    consts = [row(lnx_g), row(lnx_b), wo_bf, row(ln1_g), row(ln1_b)]
    return pl.pallas_call(
        functools.partial(_post_kernel, alpha=alpha),
        name="post",
        grid=(T // tt,),
        in_specs=[tile(aw), tile(rw), tile(rw), tile(rw), tile(D),
                  pl.BlockSpec((1, 6, D), lambda i: (i // tps, 0, 0))] + [full(a) for a in consts],
        out_specs=[tile(D), pl.BlockSpec((tt, D // LANES, LANES), lambda i: (i, 0, 0))],
        out_shape=[jax.ShapeDtypeStruct((T, D), F32), jax.ShapeDtypeStruct((T, D // LANES, LANES), F32)],
        compiler_params=_cparams(1),
    )(ya, ys, bonus, g, x2, mod3, *consts)


def _top_rows(s, count, payload=None):
    nrow = -lax.broadcasted_iota(jnp.int32, s.shape, 0).astype(F32)
    vals, pays = [], []
    for _ in range(count):
        m = jnp.max(s, axis=0, keepdims=True)
        first = jnp.max(jnp.where(s == m, nrow, -jnp.inf), axis=0, keepdims=True)
        hit = nrow == first
        vals.append(m)
        pays.append(-first if payload is None else jnp.sum(jnp.where(hit, payload, 0.0), axis=0, keepdims=True))
        s = jnp.where(hit, -jnp.inf, s)
    return vals, pays


def _pkeys_kernel(h_ref, wq_ref, sk_ref, row_o, sh_o, gate_o, *, heads, nkeys, half):
    q = _dot(_load_tiles(h_ref).astype(BF16), wq_ref[...])
    L = q.shape[0]
    j8 = lax.broadcasted_iota(jnp.int32, (SUBLANES, L), 0)
    ninf = -jnp.inf
    fvs, fes = [], []
    for h in range(heads):
        sv, si = [], []
        for c in range(2):
            qc = q[:, (2 * h + c) * half:(2 * h + c + 1) * half].astype(BF16)
            v_, i_ = _top_rows(_dot_nt(sk_ref[2 * h + c], qc), TOPK)
            sv.append(v_)
            si.append(i_)
        sv1 = jnp.concatenate(sv[1], axis=0)
        si1 = jnp.concatenate(si[1], axis=0)
        cv = [sv[0][0] + sv1]
        ce = [si[0][0] * nkeys + si1]
        for i in range(1, SUBLANES):
            cv.append(jnp.where(j8 < TOPK // (i + 1), sv[0][i] + sv1[:SUBLANES], ninf))
            ce.append(si[0][i] * nkeys + si1[:SUBLANES])
        cv.append(jnp.concatenate(sv[0][SUBLANES:], axis=0) + sv1[0:1])
        ce.append(jnp.concatenate(si[0][SUBLANES:], axis=0) * nkeys + si1[0:1])
        fv, fe = _top_rows(jnp.concatenate(cv, axis=0), TOPK, jnp.concatenate(ce, axis=0))
        fvs.append(jnp.concatenate(fv, axis=0))
        fes.append(jnp.concatenate(fe, axis=0))
    fv = jnp.concatenate(fvs, axis=0).T
    fe = jnp.concatenate(fes, axis=0).T
    n_out = heads * TOPK
    same = jnp.where(lax.broadcasted_iota(jnp.int32, (n_out, n_out), 0) // TOPK
                     == lax.broadcasted_iota(jnp.int32, (n_out, n_out), 1) // TOPK,
                     1.0, 0.0).astype(BF16)
    top = _dot_exact_rhs(jnp.where(lax.broadcasted_iota(jnp.int32, fv.shape, 1) % TOPK == 0, fv, 0.0), same)
    e = jnp.exp(fv - top)
    gate_o[...] = e / _dot_exact_rhs(e, same)
    pairs = float(nkeys * nkeys // 2)
    high = fe < pairs
    row_o[...] = (jnp.where(high, fe, fe - pairs) * SUBLANES).astype(jnp.int32)
    sh_o[...] = jnp.where(high, 0, 16).astype(jnp.int32)


def _pkeys(h3, wq_bf, sk_bf, heads, nkeys, half):
    T = h3.shape[0]
    tt = min(256, T)
    n_out = heads * TOPK
    out_spec = pl.BlockSpec((tt, n_out), lambda i: (i, 0))
    return pl.pallas_call(
        functools.partial(_pkeys_kernel, heads=heads, nkeys=nkeys, half=half),
        name="pkeys",
        grid=(T // tt,),
        in_specs=[pl.BlockSpec((tt,) + h3.shape[1:], lambda i: (i, 0, 0)),
                  pl.BlockSpec(wq_bf.shape, lambda i: (0, 0)),
                  pl.BlockSpec(sk_bf.shape, lambda i: (0, 0, 0))],
        out_specs=[out_spec] * 3,
        out_shape=[jax.ShapeDtypeStruct((T, n_out), jnp.int32),
                   jax.ShapeDtypeStruct((T, n_out), jnp.int32),
                   jax.ShapeDtypeStruct((T, n_out), F32)],
        compiler_params=_cparams(1),
    )(h3, wq_bf, sk_bf)


PEER_TOK = 256
PEER_UNROLL = 8


def _pack_table(tab):
    n, d = tab.shape
    assert d == SUBLANES * LANES and n % 2 == 0
    pairs = n // 2
    tr = min(512, pairs)
    packed = pl.pallas_call(
        _pack_kernel,
        name="pack",
        grid=(pairs // tr,),
        in_specs=[pl.BlockSpec((tr, d), lambda i: (i, 0)),
                  pl.BlockSpec((tr, d), lambda i: (i + pairs // tr, 0))],
        out_specs=pl.BlockSpec((tr, SUBLANES, LANES), lambda i: (i, 0, 0)),
        out_shape=jax.ShapeDtypeStruct((pairs, SUBLANES, LANES), jnp.uint32),
        compiler_params=_cparams(1),
    )(tab, tab)
    return packed.reshape(pairs * SUBLANES, LANES)


def _pack_kernel(hi_ref, lo_ref, o_ref):
    hi = lax.bitcast_convert_type(hi_ref[...].astype(BF16).astype(F32), jnp.uint32)
    lo = lax.bitcast_convert_type(lo_ref[...].astype(BF16).astype(F32), jnp.uint32)
    _store_tiles(o_ref, hi | (lo >> 16))


GROUP = 32
SLOTS = 2 * SUBLANES


def _expand_matrix(n_sel):
    r = lax.broadcasted_iota(jnp.int32, (2 * n_sel, n_sel * SLOTS), 0)
    c = lax.broadcasted_iota(jnp.int32, (2 * n_sel, n_sel * SLOTS), 1)
    return jnp.where((r % n_sel == c // SLOTS) & (r // n_sel == c % 2), 1.0, 0.0).astype(BF16)


def _sublane_mask(rows, cols):
    r = lax.broadcasted_iota(jnp.int32, (rows, cols), 0)
    c = lax.broadcasted_iota(jnp.int32, (rows, cols), 1)
    return jnp.where(r % SUBLANES == (c % SLOTS) // 2, 1.0, 0.0).astype(F32)


def _gather_tiles(tab_ref, row_ref, t, first, scale=None):
    half = GROUP // 2

    def tile(k):
        w = pltpu.bitcast(tab_ref[pl.ds(pl.multiple_of(row_ref[t, k], SUBLANES), SUBLANES), :], BF16)
        return w if scale is None else w * scale

    a = jnp.concatenate([tile(first + i) for i in range(half)], axis=0)
    b = jnp.concatenate([tile(first + half + i) for i in range(half)], axis=0)
    return jnp.concatenate([a, b], axis=1)


def _slot_mask(rows, cols):
    r = lax.broadcasted_iota(jnp.int32, (rows, cols), 0)
    c = lax.broadcasted_iota(jnp.int32, (rows, cols), 1)
    return jnp.where((c // SLOTS) % rows == r, 1.0, 0.0).astype(F32)


def _peer_u_kernel(row_ref, h_ref, gate_ref, sh_ref, e_ref, tab_ref, o_ref, z_ref, sel_ref, m_ref, *, n_sel):
    gw = (GROUP // 2) * SLOTS
    half = GROUP // 2
    width = n_sel * SLOTS
    high = sh_ref[...] == 0
    pick = jnp.concatenate([jnp.where(high, 0.0, 1.0), jnp.where(high, 1.0, 0.0)], axis=1).astype(BF16)
    sel_ref[...] = _dot(pick, e_ref[...])
    m_ref[...] = _slot_mask(half, width)

    def token(t, carry):
        hb = lax.bitcast_convert_type(h_ref[t].astype(BF16).astype(F32), jnp.uint32)
        h16 = pltpu.bitcast(hb | (hb >> 16), BF16)
        l16 = (jnp.broadcast_to(sel_ref[pl.ds(t, 1), :], (half, width)) * m_ref[...]).astype(BF16)
        parts = []
        for g in range(n_sel // GROUP):
            lhs = jnp.concatenate([l16[:, 2 * g * gw:(2 * g + 1) * gw],
                                   l16[:, (2 * g + 1) * gw:(2 * g + 2) * gw]], axis=0)
            res = _dot(lhs, _gather_tiles(tab_ref, row_ref, t, g * GROUP, h16))
            parts += [res[0:half, :LANES], res[half:GROUP, LANES:]]
        part = jnp.concatenate(parts, axis=0)
        z_ref[pl.ds(t, 1), :] = jnp.sum(part.T, axis=0, keepdims=True)
        return carry

    lax.fori_loop(0, h_ref.shape[0], token, 0, unroll=PEER_UNROLL)
    s = z_ref[...]
    act = s * (lax.erf(s * (2.0 ** -0.5)) + 1.0) * 0.5
    o_ref[...] = act * gate_ref[...]


def _peer_v_kernel(row_ref, coef_ref, sh_ref, e_ref, tab_ref, o_ref, la_ref, lb_ref, m_ref, *, n_sel):
    gw = (GROUP // 2) * SLOTS
    c = coef_ref[...]
    high = sh_ref[...] == 0
    c2 = jnp.concatenate([jnp.where(high, 0.0, c), jnp.where(high, c, 0.0)], axis=1)
    ca = c2.astype(BF16)
    cb = (c2 - ca.astype(F32)).astype(BF16)
    la_ref[...] = _dot(ca, e_ref[...])
    lb_ref[...] = _dot(cb, e_ref[...])
    m_ref[...] = _sublane_mask(2 * SUBLANES, n_sel * SLOTS)

    def token(t, carry):
        width = n_sel * SLOTS
        la = jnp.broadcast_to(la_ref[pl.ds(t, 1), :], (SUBLANES, width))
        lb = jnp.broadcast_to(lb_ref[pl.ds(t, 1), :], (SUBLANES, width))
        l16 = (jnp.concatenate([la, lb], axis=0) * m_ref[...]).astype(BF16)
        acc = None
        for g in range(n_sel // GROUP):
            lhs = jnp.concatenate([l16[:, 2 * g * gw:(2 * g + 1) * gw],
                                   l16[:, (2 * g + 1) * gw:(2 * g + 2) * gw]], axis=0)
            res = _dot(lhs, _gather_tiles(tab_ref, row_ref, t, g * GROUP))
            acc = res if acc is None else acc + res
        o_ref[t] = (acc[0:8, :LANES] + acc[8:16, :LANES]) + (acc[16:24, LANES:] + acc[24:32, LANES:])
        return carry

    lax.fori_loop(0, o_ref.shape[0], token, 0, unroll=PEER_UNROLL)


def _table_spec(tab):
    return pl.BlockSpec(tab.shape, lambda i: (0, 0), pipeline_mode=pl.Buffered(1))


def _smem_tile(tt, width):
    return pl.BlockSpec((tt, width), lambda i: (i, 0), memory_space=pltpu.SMEM)


def _peer_u(rows, shifts, h3, gates, tab):
    T, n_sel = rows.shape
    dd = h3.shape[1]
    assert dd == SUBLANES and n_sel % GROUP == 0
    tt = min(PEER_TOK, T)
    e = _expand_matrix(n_sel)
    tile = pl.BlockSpec((tt, n_sel), lambda i: (i, 0))
    return pl.pallas_call(
        functools.partial(_peer_u_kernel, n_sel=n_sel),
        name="peer_u",
        grid=(T // tt,),
        in_specs=[_smem_tile(tt, n_sel),
                  pl.BlockSpec((tt, dd, LANES), lambda i: (i, 0, 0)),
                  tile, tile,
                  pl.BlockSpec(e.shape, lambda i: (0, 0)),
                  _table_spec(tab)],
        out_specs=tile,
        out_shape=jax.ShapeDtypeStruct((T, n_sel), F32),
        scratch_shapes=[pltpu.VMEM((tt, n_sel), F32), pltpu.VMEM((tt, n_sel * SLOTS), F32),
                        pltpu.VMEM((GROUP // 2, n_sel * SLOTS), F32)],
        compiler_params=_cparams(1),
    )(rows, h3, gates, shifts, e, tab)


def _peer_v(rows, shifts, coef, tab):
    T, n_sel = rows.shape
    dd = SUBLANES
    assert n_sel % GROUP == 0
    tt = min(PEER_TOK, T)
    e = _expand_matrix(n_sel)
    tile = pl.BlockSpec((tt, n_sel), lambda i: (i, 0))
    return pl.pallas_call(
        functools.partial(_peer_v_kernel, n_sel=n_sel),
        name="peer_v",
        grid=(T // tt,),
        in_specs=[_smem_tile(tt, n_sel), tile, tile, pl.BlockSpec(e.shape, lambda i: (0, 0)), _table_spec(tab)],
        out_specs=pl.BlockSpec((tt, dd, LANES), lambda i: (i, 0, 0)),
        out_shape=jax.ShapeDtypeStruct((T, dd, LANES), F32),
        scratch_shapes=[pltpu.VMEM((tt, n_sel * SLOTS), F32), pltpu.VMEM((tt, n_sel * SLOTS), F32),
                        pltpu.VMEM((2 * SUBLANES, n_sel * SLOTS), F32)],
        compiler_params=_cparams(1),
    )(rows, coef, shifts, e, tab)


def _ln2_kernel(x1_ref, y_ref, mod_ref, g_ref, b_ref, o_ref, *, alpha):
    m = mod_ref[0]
    o_ref[...] = _layer_norm(alpha * x1_ref[...] + m[5:6] * _load_tiles(y_ref), g_ref[...], b_ref[...])


def _ln2(x1, y3, mod3, g, b, S, alpha):
    T, D = x1.shape
    tt = min(512, S)
    tps = S // tt
    tile = pl.BlockSpec((tt, D), lambda i: (i, 0))
    vec = pl.BlockSpec((1, D), lambda i: (0, 0))
    return pl.pallas_call(
        functools.partial(_ln2_kernel, alpha=alpha),
        name="ln2",
        grid=(T // tt,),
        in_specs=[tile, pl.BlockSpec((tt,) + y3.shape[1:], lambda i: (i, 0, 0)),
                  pl.BlockSpec((1, 6, D), lambda i: (i // tps, 0, 0)), vec, vec],
        out_specs=tile,
        out_shape=jax.ShapeDtypeStruct((T, D), F32),
        compiler_params=_cparams(1),
    )(x1, y3, mod3, g.reshape(1, D), b.reshape(1, D))


def kernel(x, c, ada_w, ada_b, w_in, attn_sink, shift_mu, decay_w0, decay_w2, iclr_a0, iclr_a2, gate_g2, k_k, k_a,
           r_k, lnx_g, lnx_b, w_out, ln1_g, ln1_b, peer_wq, peer_subkeys, peer_u, peer_v, ln2_g, ln2_b):
    B, S, D = x.shape
    depth = ada_w.shape[0]
    alpha = (2.0 * depth) ** 0.25
    heads = attn_sink.shape[1]
    rheads = r_k.shape[1]
    rw = rheads * HEAD_DIM
    aw = heads * HEAD_DIM
    kw = (w_in.shape[2] - aw - shift_mu.shape[1]) // 2
    kv_heads = kw // HEAD_DIM
    pheads, _, nkeys, half = peer_subkeys.shape[1:]
    assert S % WINDOW == 0 and S % CHUNK == 0 and (B * S) % PEER_TOK == 0
    T = B * S
    xt = x.reshape(T, D)
    for l in range(depth):
        mod3 = _ada(c, ada_w[l], ada_b[l]).reshape(B, 6, D)
        q, k, v, pr = _proj(xt, mod3, w_in[l].astype(BF16), S, aw, kw)
        ya = _attn(attn_sink[l], q, k, v, B, S, heads, kv_heads)
        r, lw, kr, vr, kk, bb, g, bonus = _prep(pr, S, shift_mu[l], decay_w0[l], decay_w2[l], iclr_a0[l],
                                                iclr_a2[l], gate_g2[l], k_k[l], k_a[l], r_k[l].reshape(-1))
        ys = _scan(r, lw, kr, vr, kk, bb, S, rheads)
        x1, h3 = _post(ya, ys, bonus, g, xt, mod3, lnx_g[l], lnx_b[l], w_out[l].astype(BF16), ln1_g[l], ln1_b[l],
                       S, alpha)
        rows, shifts, gates = _pkeys(h3, peer_wq[l].astype(BF16),
                                     peer_subkeys[l].reshape(pheads * 2, nkeys, half).astype(BF16),
                                     pheads, nkeys, half)
        coef = _peer_u(rows, shifts, h3, gates, _pack_table(peer_u[l]))
        y3 = _peer_v(rows, shifts, coef, _pack_table(peer_v[l]))
        xt = _ln2(x1, y3, mod3, ln2_g[l], ln2_b[l], S, alpha)
    return xt.reshape(B, S, D)
```

```python
import functools
import math

import jax
import jax.numpy as jnp
from jax import lax
from jax.experimental import pallas as pl
from jax.experimental.pallas import tpu as pltpu

F32 = jnp.float32
BF16 = jnp.bfloat16

HEAD_DIM = 64
WINDOW = 128
TOPK = 16
LN_EPS = 1e-5
GN_EPS = 64e-5
LANES = 128
SUBLANES = 8
VMEM_LIMIT = 48 * 1024 * 1024
NEG = -1e30


def _cparams(n_axes, vmem=VMEM_LIMIT):
    return pltpu.CompilerParams(dimension_semantics=("arbitrary",) * n_axes, vmem_limit_bytes=vmem)


def _split3(x):
    hi = x.astype(BF16)
    r1 = x - hi.astype(F32)
    mid = r1.astype(BF16)
    lo = (r1 - mid.astype(F32)).astype(BF16)
    return hi, mid, lo


def _dot(a, b):
    return jnp.dot(a, b, preferred_element_type=F32)


def _dot_nt(a, b):
    return lax.dot_general(a, b, (((1,), (1,)), ((), ())), preferred_element_type=F32)


def _dot3(a, b):
    ah = a.astype(BF16)
    al = (a - ah.astype(F32)).astype(BF16)
    bh = b.astype(BF16)
    bl = (b - bh.astype(F32)).astype(BF16)
    return _dot(ah, bh) + _dot(ah, bl) + _dot(al, bh)


def _dot_exact_rhs(x, m_bf16):
    hi, mid, lo = _split3(x)
    return _dot(hi, m_bf16) + _dot(mid, m_bf16) + _dot(lo, m_bf16)


def _dot_exact_lhs(m_bf16, x):
    hi, mid, lo = _split3(x)
    return _dot(m_bf16, hi) + _dot(m_bf16, mid) + _dot(m_bf16, lo)


def _head_ones(width):
    r = lax.broadcasted_iota(jnp.int32, (width, width), 0) // HEAD_DIM
    c = lax.broadcasted_iota(jnp.int32, (width, width), 1) // HEAD_DIM
    return jnp.where(r == c, 1.0, 0.0).astype(BF16)


def _ada_kernel(c_ref, w_ref, b_ref, o_ref):
    c = c_ref[...]
    cs = c * jax.nn.sigmoid(c)
    o_ref[...] = _dot3(cs, w_ref[...]) + b_ref[...]


def _ada(c, w, b):
    B, D = c.shape
    N = w.shape[1]
    tn = 1536 if N % 1536 == 0 else N
    return pl.pallas_call(
        _ada_kernel,
        name="ada",
        grid=(N // tn,),
        in_specs=[pl.BlockSpec((B, D), lambda j: (0, 0)),
                  pl.BlockSpec((D, tn), lambda j: (0, j)),
                  pl.BlockSpec((1, tn), lambda j: (0, j))],
        out_specs=pl.BlockSpec((B, tn), lambda j: (0, j)),
        out_shape=jax.ShapeDtypeStruct((B, N), F32),
        compiler_params=_cparams(1),
    )(c, w, b.reshape(1, N))


def _proj_kernel(x_ref, mod_ref, w_ref, mu_ref, w0_ref, w2_ref, a0_ref, a2_ref, g2_ref, kk_ref, ka_ref, rk_ref,
                 q_o, k_o, v_o, r_o, lw_o, kr_o, vr_o, kkr_o, b_o, g_o, bonus_o, carry_ref, *, aw, kw, tps):
    m = mod_ref[0]
    h = x_ref[...] * (1.0 + m[1:2]) + m[0:1]
    p = _dot(h.astype(BF16), w_ref[...])
    q_o[...] = (p[:, :aw] * (HEAD_DIM ** -0.5)).astype(BF16)
    k_o[...] = p[:, aw:aw + kw].astype(BF16)
    v_o[...] = p[:, aw + kw:aw + 2 * kw].astype(BF16)
    pr = p[:, aw + 2 * kw:]
    last = jnp.where(lax.rem(pl.program_id(0), tps) > 0, carry_ref[...], 0.0)
    carry_ref[...] = pr[pr.shape[0] - 1:, :]
    outs = _rwkv_prep(pr, last, mu_ref[...], w0_ref[...], w2_ref[...], a0_ref[...], a2_ref[...], g2_ref[...],
                      kk_ref[...], ka_ref[...], rk_ref[...])
    for o_ref, val in zip((r_o, lw_o, kr_o, vr_o, kkr_o, b_o, g_o, bonus_o), outs):
        o_ref[...] = val


def _proj(x2, mod3, w_bf, S, aw, kw, mu, w0, w2, a0, a2, g2, k_k, k_a, r_k):
    T, D = x2.shape
    N = w_bf.shape[1]
    cols = N - aw - 2 * kw
    rw = w0.shape[-1]
    tm = min(256, S)
    tps = S // tm
    row = lambda a: a.reshape(1, -1)
    full = lambda a: pl.BlockSpec(a.shape, lambda i: (0,) * a.ndim)
    tile = lambda w: pl.BlockSpec((tm, w), lambda i: (i, 0))
    consts = [w_bf, row(mu), row(w0), w2, row(a0), a2, g2, row(k_k), row(k_a), row(r_k)]
    return pl.pallas_call(
        functools.partial(_proj_kernel, aw=aw, kw=kw, tps=tps),
        name="proj",
        grid=(T // tm,),
        in_specs=[tile(D), pl.BlockSpec((1, 6, D), lambda i: (i // tps, 0, 0))] + [full(a) for a in consts],
        out_specs=[tile(aw), tile(kw), tile(kw)] + [tile(rw)] * 8,
        out_shape=[jax.ShapeDtypeStruct((T, aw), BF16),
                   jax.ShapeDtypeStruct((T, kw), BF16),
                   jax.ShapeDtypeStruct((T, kw), BF16)] + [jax.ShapeDtypeStruct((T, rw), F32)] * 8,
        scratch_shapes=[pltpu.VMEM((1, cols), F32)],
        compiler_params=_cparams(1),
    )(x2, mod3, *consts)


def _attn_kernel(sink_ref, q_ref, kp_ref, kc_ref, vp_ref, vc_ref, o_ref, *, heads, kv_heads):
    n = pl.program_id(1)
    groups = heads // kv_heads
    qi = lax.broadcasted_iota(jnp.int32, (WINDOW, 2 * WINDOW), 0)
    kj = lax.broadcasted_iota(jnp.int32, (WINDOW, 2 * WINDOW), 1)
    mask = (kj > qi) & (kj <= qi + WINDOW) & ((kj >= WINDOW) | (n > 0))
    mask = jnp.concatenate([mask] * groups, axis=0)
    grp = lax.broadcasted_iota(jnp.int32, (groups * WINDOW, 1), 0) // WINDOW
    kvs = range(kv_heads)
    ks = [slice(kh * HEAD_DIM, (kh + 1) * HEAD_DIM) for kh in kvs]
    kwin = [jnp.concatenate([kp_ref[:, s], kc_ref[:, s]], axis=0) for s in ks]
    vwin = [jnp.concatenate([vp_ref[:, s], vc_ref[:, s]], axis=0) for s in ks]
    hs = lambda kh, g: slice((kh * groups + g) * HEAD_DIM, (kh * groups + g + 1) * HEAD_DIM)
    qs = [jnp.concatenate([q_ref[:, hs(kh, g)] for g in range(groups)], axis=0) for kh in kvs]
    sk = []
    for kh in kvs:
        col = jnp.full((groups * WINDOW, 1), sink_ref[kh * groups], F32)
        for g in range(1, groups):
            col = jnp.where(grp == g, sink_ref[kh * groups + g], col)
        sk.append(col)
    s = [jnp.where(mask, _dot_nt(qs[kh], kwin[kh]), NEG) for kh in kvs]
    m = [jnp.maximum(jnp.max(s[kh], axis=1, keepdims=True), sk[kh]) for kh in kvs]
    p = [jnp.exp(s[kh] - m[kh]) for kh in kvs]
    den = [jnp.sum(p[kh], axis=1, keepdims=True) + jnp.exp(sk[kh] - m[kh]) for kh in kvs]
    o = [_dot(p[kh].astype(BF16), vwin[kh]) / den[kh] for kh in kvs]
    for kh in kvs:
        for g in range(groups):
            o_ref[:, hs(kh, g)] = o[kh][g * WINDOW:(g + 1) * WINDOW].astype(o_ref.dtype)


def _attn(sink, q, k, v, B, S, heads, kv_heads):
    T, aw = q.shape
    kw = k.shape[1]
    nb = S // WINDOW
    cur = lambda b, n: (b * nb + n, 0)
    prev = lambda b, n: (b * nb + jnp.maximum(n - 1, 0), 0)
    return pl.pallas_call(
        functools.partial(_attn_kernel, heads=heads, kv_heads=kv_heads),
        name="attn",
        grid=(B, nb),
        in_specs=[pl.BlockSpec(memory_space=pltpu.SMEM),
                  pl.BlockSpec((WINDOW, aw), cur),
                  pl.BlockSpec((WINDOW, kw), prev),
                  pl.BlockSpec((WINDOW, kw), cur),
                  pl.BlockSpec((WINDOW, kw), prev),
                  pl.BlockSpec((WINDOW, kw), cur)],
        out_specs=pl.BlockSpec((WINDOW, aw), cur),
        out_shape=jax.ShapeDtypeStruct((T, aw), BF16),
        compiler_params=_cparams(2),
    )(sink, q, k, k, v, v)


def _softplus(z):
    return jnp.maximum(z, 0.0) + jnp.log1p(jnp.exp(-jnp.abs(z)))


def _rwkv_prep(p, last, mu, w0, w2, a0, a2, g2, k_k, k_a, r_k):
    rw = w0.shape[-1]
    dl, al = w2.shape[0], a2.shape[0]
    row = lax.broadcasted_iota(jnp.int32, p.shape, 0)
    p_prev = jnp.where(row == 0, last, pltpu.roll(p, 1, 0))
    p = p + (p_prev - p) * mu
    r = p[:, 0:rw]
    k = p[:, rw:2 * rw]
    v = p[:, 2 * rw:3 * rw]
    wd = p[:, 3 * rw:3 * rw + dl]
    ad = p[:, 3 * rw + dl:3 * rw + dl + al]
    gd = p[:, 3 * rw + dl + al:]
    ones = _head_ones(rw)
    w_log = -_softplus(-(w0 + _dot3(jnp.tanh(wd), w2))) - 0.5
    a = jax.nn.sigmoid(a0 + _dot3(ad, a2))
    g = _dot3(jax.nn.sigmoid(gd), g2)
    kk = k * k_k
    kk = kk * lax.rsqrt(jnp.maximum(_dot_exact_rhs(kk * kk, ones), 1e-24))
    k = k * (1.0 + (a - 1.0) * k_a)
    bonus = _dot_exact_rhs(r * k * r_k, ones) * v
    return r, -jnp.exp(w_log), k, v, kk, kk * a, g, bonus


CHUNK = 64


def _scan_kernel(r_ref, lw_ref, k_ref, v_ref, kk_ref, b_ref, y_ref, st_ref, *, heads):
    C = CHUNK

    @pl.when(pl.program_id(1) == 0)
    def _():
        st_ref[...] = jnp.zeros_like(st_ref)

    ri = lax.broadcasted_iota(jnp.int32, (C, C), 0)
    ci = lax.broadcasted_iota(jnp.int32, (C, C), 1)
    incl = ri >= ci
    strict = ri > ci
    tri = jnp.where(incl, 1.0, 0.0).astype(BF16)
    nseq = r_ref.shape[0]
    rt, nt, bt, kt, vb, pe = [], [], [], [], [], []
    for s in range(nseq):
        lw = lw_ref[s]
        cum = _dot_exact_lhs(tri, lw)
        pe.append(jnp.exp(cum))
        pinv = jnp.exp(-cum)
        rt.append((r_ref[s] * pe[s]).astype(BF16))
        nt.append((-kk_ref[s] * jnp.exp(cum - lw)).astype(BF16))
        bt.append((b_ref[s] * pinv).astype(BF16))
        kt.append((k_ref[s] * pinv).astype(BF16))
        vb.append(v_ref[s].astype(BF16))
    units = [(s, h) for s in range(nseq) for h in range(heads)]
    un = range(len(units))
    hs = lambda h: slice(h * HEAD_DIM, (h + 1) * HEAD_DIM)
    nr = [jnp.concatenate([nt[s][:, hs(h)], rt[s][:, hs(h)]], axis=0) for s, h in units]
    bh = [bt[s][:, hs(h)] for s, h in units]
    kh = [kt[s][:, hs(h)] for s, h in units]
    vh = [vb[s][:, hs(h)] for s, h in units]
    ab = [_dot_nt(nr[i], bh[i]) for i in un]
    ak = [_dot_nt(nr[i], kh[i]) for i in un]
    s0 = [st_ref[i] for i in un]
    ws = [_dot_nt(nr[i], s0[i].astype(BF16)) for i in un]
    u = [ws[i][:C] + _dot(jnp.where(strict, ak[i][:C], 0.0).astype(BF16), vh[i]) for i in un]
    ap = [jnp.where(strict, ab[i][:C], 0.0) for i in un]
    steps = int(math.log2(C))
    for j in range(steps):
        apb = [a.astype(BF16) for a in ap]
        u = [u[i] + _dot(apb[i], u[i].astype(BF16)) for i in un]
        if j + 1 < steps:
            ap = [_dot(a, a) for a in apb]
    ub = [x.astype(BF16) for x in u]
    y = [ws[i][C:] + _dot(jnp.where(incl, ab[i][C:], 0.0).astype(BF16), ub[i])
         + _dot(jnp.where(incl, ak[i][C:], 0.0).astype(BF16), vh[i]) for i in un]
    for i, (s, h) in enumerate(units):
        y_ref[s, :, hs(h)] = y[i]
    uvt = [jnp.concatenate([u[i], v_ref[s, :, hs(h)]], axis=0).T.astype(BF16) for i, (s, h) in enumerate(units)]
    sn = [s0[i] + _dot(uvt[i], jnp.concatenate([bh[i], kh[i]], axis=0)) for i in un]
    for i, (s, h) in enumerate(units):
        st_ref[i] = sn[i] * pe[s][C - 1:C, hs(h)]


SCAN_SEQS = 4


def _scan(r, lw, k, v, kk, b, S, heads):
    T, rw = r.shape
    B = T // S
    nseq = SCAN_SEQS if B % SCAN_SEQS == 0 else 1
    nch = S // CHUNK
    tile = pl.BlockSpec((nseq, CHUNK, rw), lambda bb, c: (bb, c, 0))
    seq = lambda a: a.reshape(B, S, rw)
    return pl.pallas_call(
        functools.partial(_scan_kernel, heads=heads),
        name="scan",
        grid=(B // nseq, nch),
        in_specs=[tile] * 6,
        out_specs=tile,
        out_shape=jax.ShapeDtypeStruct((B, S, rw), F32),
        scratch_shapes=[pltpu.VMEM((nseq * heads, HEAD_DIM, HEAD_DIM), F32)],
        compiler_params=_cparams(2),
    )(seq(r), seq(lw), seq(k), seq(v), seq(kk), seq(b)).reshape(T, rw)


def _layer_norm(z, g, b):
    mu = jnp.mean(z, axis=-1, keepdims=True)
    d = z - mu
    var = jnp.mean(d * d, axis=-1, keepdims=True)
    return d * lax.rsqrt(var + LN_EPS) * g + b


def _store_tiles(ref3, x):
    for s in range(ref3.shape[1]):
        ref3[:, s, :] = x[:, s * LANES:(s + 1) * LANES]


def _load_tiles(ref3):
    return jnp.concatenate([ref3[:, s, :] for s in range(ref3.shape[1])], axis=1)


def _post_kernel(ya_ref, ys_ref, bonus_ref, g_ref, x_ref, mod_ref, lg_ref, lb_ref, wo_ref, n1g_ref, n1b_ref,
                 x1_o, h2_o, *, alpha):
    m = mod_ref[0]
    ys = ys_ref[...]
    rw = ys.shape[1]
    aw = ya_ref.shape[1]
    ones = _head_ones(rw)
    mean = _dot_exact_rhs(ys, ones) * (1.0 / HEAD_DIM)
    d = ys - mean
    var = _dot_exact_rhs(d * d, ones) * (1.0 / HEAD_DIM)
    yr = (d * lax.rsqrt(var + GN_EPS) * lg_ref[...] + lb_ref[...] + bonus_ref[...]) * g_ref[...]
    y = _dot(ya_ref[...], wo_ref[0:aw, :]) + _dot(yr.astype(BF16), wo_ref[aw:aw + rw, :])
    x1 = _layer_norm(alpha * x_ref[...] + m[2:3] * y, n1g_ref[...], n1b_ref[...])
    x1_o[...] = x1
    _store_tiles(h2_o, x1 * (1.0 + m[4:5]) + m[3:4])


def _post(ya, ys, bonus, g, x2, mod3, lnx_g, lnx_b, wo_bf, ln1_g, ln1_b, S, alpha):
    T, D = x2.shape
    aw, rw = ya.shape[1], ys.shape[1]
    tt = min(256, S)
    tps = S // tt
    row = lambda a: a.reshape(1, -1)
    full = lambda a: pl.BlockSpec(a.shape, lambda i: (0,) * a.ndim)
    tile = lambda w: pl.BlockSpec((tt, w), lambda i: (i, 0))
    consts = [row(lnx_g), row(lnx_b), wo_bf, row(ln1_g), row(ln1_b)]
    return pl.pallas_call(
        functools.partial(_post_kernel, alpha=alpha),
        name="post",
        grid=(T // tt,),
        in_specs=[tile(aw), tile(rw), tile(rw), tile(rw), tile(D),
                  pl.BlockSpec((1, 6, D), lambda i: (i // tps, 0, 0))] + [full(a) for a in consts],
        out_specs=[tile(D), pl.BlockSpec((tt, D // LANES, LANES), lambda i: (i, 0, 0))],
        out_shape=[jax.ShapeDtypeStruct((T, D), F32), jax.ShapeDtypeStruct((T, D // LANES, LANES), F32)],
        compiler_params=_cparams(1),
    )(ya, ys, bonus, g, x2, mod3, *consts)


def _top_rows(s, count, payload=None):
    nrow = -lax.broadcasted_iota(jnp.int32, s.shape, 0).astype(F32)
    vals, pays = [], []
    for _ in range(count):
        m = jnp.max(s, axis=0, keepdims=True)
        first = jnp.max(jnp.where(s == m, nrow, -jnp.inf), axis=0, keepdims=True)
        hit = nrow == first
        vals.append(m)
        pays.append(-first if payload is None else jnp.sum(jnp.where(hit, payload, 0.0), axis=0, keepdims=True))
        s = jnp.where(hit, -jnp.inf, s)
    return vals, pays


def _pkeys_kernel(h_ref, wq_ref, sk_ref, row_o, sh_o, gate_o, *, heads, nkeys, half):
    q = _dot(_load_tiles(h_ref).astype(BF16), wq_ref[...])
    L = q.shape[0]
    j8 = lax.broadcasted_iota(jnp.int32, (SUBLANES, L), 0)
    ninf = -jnp.inf
    fvs, fes = [], []
    for h in range(heads):
        sv, si = [], []
        for c in range(2):
            qc = q[:, (2 * h + c) * half:(2 * h + c + 1) * half].astype(BF16)
            v_, i_ = _top_rows(_dot_nt(sk_ref[2 * h + c], qc), TOPK)
            sv.append(v_)
            si.append(i_)
        sv1 = jnp.concatenate(sv[1], axis=0)
        si1 = jnp.concatenate(si[1], axis=0)
        cv = [sv[0][0] + sv1]
        ce = [si[0][0] * nkeys + si1]
        for i in range(1, SUBLANES):
            cv.append(jnp.where(j8 < TOPK // (i + 1), sv[0][i] + sv1[:SUBLANES], ninf))
            ce.append(si[0][i] * nkeys + si1[:SUBLANES])
        cv.append(jnp.concatenate(sv[0][SUBLANES:], axis=0) + sv1[0:1])
        ce.append(jnp.concatenate(si[0][SUBLANES:], axis=0) * nkeys + si1[0:1])
        fv, fe = _top_rows(jnp.concatenate(cv, axis=0), TOPK, jnp.concatenate(ce, axis=0))
        fvs.append(jnp.concatenate(fv, axis=0))
        fes.append(jnp.concatenate(fe, axis=0))
    fv = jnp.concatenate(fvs, axis=0).T
    fe = jnp.concatenate(fes, axis=0).T
    n_out = heads * TOPK
    same = jnp.where(lax.broadcasted_iota(jnp.int32, (n_out, n_out), 0) // TOPK
                     == lax.broadcasted_iota(jnp.int32, (n_out, n_out), 1) // TOPK,
                     1.0, 0.0).astype(BF16)
    top = _dot_exact_rhs(jnp.where(lax.broadcasted_iota(jnp.int32, fv.shape, 1) % TOPK == 0, fv, 0.0), same)
    e = jnp.exp(fv - top)
    gate_o[...] = e / _dot_exact_rhs(e, same)
    pairs = float(nkeys * nkeys // 2)
    high = fe < pairs
    row_o[...] = (jnp.where(high, fe, fe - pairs) * SUBLANES).astype(jnp.int32)
    sh_o[...] = jnp.where(high, 0, 16).astype(jnp.int32)


def _pkeys(h3, wq_bf, sk_bf, heads, nkeys, half):
    T = h3.shape[0]
    tt = min(256, T)
    n_out = heads * TOPK
    out_spec = pl.BlockSpec((tt, n_out), lambda i: (i, 0))
    return pl.pallas_call(
        functools.partial(_pkeys_kernel, heads=heads, nkeys=nkeys, half=half),
        name="pkeys",
        grid=(T // tt,),
        in_specs=[pl.BlockSpec((tt,) + h3.shape[1:], lambda i: (i, 0, 0)),
                  pl.BlockSpec(wq_bf.shape, lambda i: (0, 0)),
                  pl.BlockSpec(sk_bf.shape, lambda i: (0, 0, 0))],
        out_specs=[out_spec] * 3,
        out_shape=[jax.ShapeDtypeStruct((T, n_out), jnp.int32),
                   jax.ShapeDtypeStruct((T, n_out), jnp.int32),
                   jax.ShapeDtypeStruct((T, n_out), F32)],
        compiler_params=_cparams(1),
    )(h3, wq_bf, sk_bf)


PEER_TOK = 256
PEER_UNROLL = 32


def _pack_table(tab):
    n, d = tab.shape
    assert d == SUBLANES * LANES and n % 2 == 0
    pairs = n // 2
    tr = min(512, pairs)
    packed = pl.pallas_call(
        _pack_kernel,
        name="pack",
        grid=(pairs // tr,),
        in_specs=[pl.BlockSpec((tr, d), lambda i: (i, 0)),
                  pl.BlockSpec((tr, d), lambda i: (i + pairs // tr, 0))],
        out_specs=pl.BlockSpec((tr, SUBLANES, LANES), lambda i: (i, 0, 0)),
        out_shape=jax.ShapeDtypeStruct((pairs, SUBLANES, LANES), jnp.uint32),
        compiler_params=_cparams(1),
    )(tab, tab)
    return packed.reshape(pairs * SUBLANES, LANES)


def _pack_kernel(hi_ref, lo_ref, o_ref):
    hi = lax.bitcast_convert_type(hi_ref[...].astype(BF16).astype(F32), jnp.uint32)
    lo = lax.bitcast_convert_type(lo_ref[...].astype(BF16).astype(F32), jnp.uint32)
    _store_tiles(o_ref, hi | (lo >> 16))


GROUP = 32
SLOTS = 2 * SUBLANES


def _expand_matrix(n_sel):
    r = lax.broadcasted_iota(jnp.int32, (2 * n_sel, n_sel * SLOTS), 0)
    c = lax.broadcasted_iota(jnp.int32, (2 * n_sel, n_sel * SLOTS), 1)
    return jnp.where((r % n_sel == c // SLOTS) & (r // n_sel == c % 2), 1.0, 0.0).astype(BF16)


def _sublane_mask(rows, cols):
    r = lax.broadcasted_iota(jnp.int32, (rows, cols), 0)
    c = lax.broadcasted_iota(jnp.int32, (rows, cols), 1)
    return jnp.where(r % SUBLANES == (c % SLOTS) // 2, 1.0, 0.0).astype(F32)


def _gather_tiles(tab_ref, row_ref, t, first, scale=None):
    half = GROUP // 2

    def tile(k):
        w = pltpu.bitcast(tab_ref[pl.ds(pl.multiple_of(row_ref[t, k], SUBLANES), SUBLANES), :], BF16)
        return w if scale is None else w * scale

    a = jnp.concatenate([tile(first + i) for i in range(half)], axis=0)
    b = jnp.concatenate([tile(first + half + i) for i in range(half)], axis=0)
    return jnp.concatenate([a, b], axis=1)


def _slot_mask(rows, cols):
    r = lax.broadcasted_iota(jnp.int32, (rows, cols), 0)
    c = lax.broadcasted_iota(jnp.int32, (rows, cols), 1)
    return jnp.where((c // SLOTS) % rows == r, 1.0, 0.0).astype(F32)


def _peer_u_kernel(row_ref, h_ref, gate_ref, sh_ref, e_ref, tab_ref, o_ref, z_ref, sel_ref, m_ref, *, n_sel):
    gw = (GROUP // 2) * SLOTS
    half = GROUP // 2
    width = n_sel * SLOTS
    high = sh_ref[...] == 0
    pick = jnp.concatenate([jnp.where(high, 0.0, 1.0), jnp.where(high, 1.0, 0.0)], axis=1).astype(BF16)
    sel_ref[...] = _dot(pick, e_ref[...])
    m_ref[...] = _slot_mask(half, width)

    def token(t, carry):
        hb = lax.bitcast_convert_type(h_ref[t].astype(BF16).astype(F32), jnp.uint32)
        h16 = pltpu.bitcast(hb | (hb >> 16), BF16)
        l16 = (jnp.broadcast_to(sel_ref[pl.ds(t, 1), :], (half, width)) * m_ref[...]).astype(BF16)
        parts = []
        for g in range(n_sel // GROUP):
            lhs = jnp.concatenate([l16[:, 2 * g * gw:(2 * g + 1) * gw],
                                   l16[:, (2 * g + 1) * gw:(2 * g + 2) * gw]], axis=0)
            res = _dot(lhs, _gather_tiles(tab_ref, row_ref, t, g * GROUP, h16))
            parts += [res[0:half, :LANES], res[half:GROUP, LANES:]]
        part = jnp.concatenate(parts, axis=0)
        z_ref[pl.ds(t, 1), :] = jnp.sum(part.T, axis=0, keepdims=True)
        return carry

    lax.fori_loop(0, h_ref.shape[0], token, 0, unroll=PEER_UNROLL)
    s = z_ref[...]
    act = s * (lax.erf(s * (2.0 ** -0.5)) + 1.0) * 0.5
    o_ref[...] = act * gate_ref[...]


def _peer_v_kernel(row_ref, coef_ref, sh_ref, e_ref, tab_ref, o_ref, la_ref, lb_ref, m_ref, *, n_sel):
    gw = (GROUP // 2) * SLOTS
    c = coef_ref[...]
    high = sh_ref[...] == 0
    c2 = jnp.concatenate([jnp.where(high, 0.0, c), jnp.where(high, c, 0.0)], axis=1)
    ca = c2.astype(BF16)
    cb = (c2 - ca.astype(F32)).astype(BF16)
    la_ref[...] = _dot(ca, e_ref[...])
    lb_ref[...] = _dot(cb, e_ref[...])
    m_ref[...] = _sublane_mask(2 * SUBLANES, n_sel * SLOTS)

    def token(t, carry):
        width = n_sel * SLOTS
        la = jnp.broadcast_to(la_ref[pl.ds(t, 1), :], (SUBLANES, width))
        lb = jnp.broadcast_to(lb_ref[pl.ds(t, 1), :], (SUBLANES, width))
        l16 = (jnp.concatenate([la, lb], axis=0) * m_ref[...]).astype(BF16)
        acc = None
        for g in range(n_sel // GROUP):
            lhs = jnp.concatenate([l16[:, 2 * g * gw:(2 * g + 1) * gw],
                                   l16[:, (2 * g + 1) * gw:(2 * g + 2) * gw]], axis=0)
            res = _dot(lhs, _gather_tiles(tab_ref, row_ref, t, g * GROUP))
            acc = res if acc is None else acc + res
        o_ref[t] = (acc[0:8, :LANES] + acc[8:16, :LANES]) + (acc[16:24, LANES:] + acc[24:32, LANES:])
        return carry

    lax.fori_loop(0, o_ref.shape[0], token, 0, unroll=PEER_UNROLL)


def _table_spec(tab):
    return pl.BlockSpec(tab.shape, lambda i: (0, 0), pipeline_mode=pl.Buffered(1))


def _smem_tile(tt, width):
    return pl.BlockSpec((tt, width), lambda i: (i, 0), memory_space=pltpu.SMEM)


def _peer_u(rows, shifts, h3, gates, tab):
    T, n_sel = rows.shape
    dd = h3.shape[1]
    assert dd == SUBLANES and n_sel % GROUP == 0
    tt = min(PEER_TOK, T)
    e = _expand_matrix(n_sel)
    tile = pl.BlockSpec((tt, n_sel), lambda i: (i, 0))
    return pl.pallas_call(
        functools.partial(_peer_u_kernel, n_sel=n_sel),
        name="peer_u",
        grid=(T // tt,),
        in_specs=[_smem_tile(tt, n_sel),
                  pl.BlockSpec((tt, dd, LANES), lambda i: (i, 0, 0)),
                  tile, tile,
                  pl.BlockSpec(e.shape, lambda i: (0, 0)),
                  _table_spec(tab)],
        out_specs=tile,
        out_shape=jax.ShapeDtypeStruct((T, n_sel), F32),
        scratch_shapes=[pltpu.VMEM((tt, n_sel), F32), pltpu.VMEM((tt, n_sel * SLOTS), F32),
                        pltpu.VMEM((GROUP // 2, n_sel * SLOTS), F32)],
        compiler_params=_cparams(1),
    )(rows, h3, gates, shifts, e, tab)


def _peer_v(rows, shifts, coef, tab):
    T, n_sel = rows.shape
    dd = SUBLANES
    assert n_sel % GROUP == 0
    tt = min(PEER_TOK, T)
    e = _expand_matrix(n_sel)
    tile = pl.BlockSpec((tt, n_sel), lambda i: (i, 0))
    return pl.pallas_call(
        functools.partial(_peer_v_kernel, n_sel=n_sel),
        name="peer_v",
        grid=(T // tt,),
        in_specs=[_smem_tile(tt, n_sel), tile, tile, pl.BlockSpec(e.shape, lambda i: (0, 0)), _table_spec(tab)],
        out_specs=pl.BlockSpec((tt, dd, LANES), lambda i: (i, 0, 0)),
        out_shape=jax.ShapeDtypeStruct((T, dd, LANES), F32),
        scratch_shapes=[pltpu.VMEM((tt, n_sel * SLOTS), F32), pltpu.VMEM((tt, n_sel * SLOTS), F32),
                        pltpu.VMEM((2 * SUBLANES, n_sel * SLOTS), F32)],
        compiler_params=_cparams(1),
    )(rows, coef, shifts, e, tab)


def _ln2_kernel(x1_ref, y_ref, mod_ref, g_ref, b_ref, o_ref, *, alpha):
    m = mod_ref[0]
    o_ref[...] = _layer_norm(alpha * x1_ref[...] + m[5:6] * _load_tiles(y_ref), g_ref[...], b_ref[...])


def _ln2(x1, y3, mod3, g, b, S, alpha):
    T, D = x1.shape
    tt = min(512, S)
    tps = S // tt
    tile = pl.BlockSpec((tt, D), lambda i: (i, 0))
    vec = pl.BlockSpec((1, D), lambda i: (0, 0))
    return pl.pallas_call(
        functools.partial(_ln2_kernel, alpha=alpha),
        name="ln2",
        grid=(T // tt,),
        in_specs=[tile, pl.BlockSpec((tt,) + y3.shape[1:], lambda i: (i, 0, 0)),
                  pl.BlockSpec((1, 6, D), lambda i: (i // tps, 0, 0)), vec, vec],
        out_specs=tile,
        out_shape=jax.ShapeDtypeStruct((T, D), F32),
        compiler_params=_cparams(1),
    )(x1, y3, mod3, g.reshape(1, D), b.reshape(1, D))


def kernel(x, c, ada_w, ada_b, w_in, attn_sink, shift_mu, decay_w0, decay_w2, iclr_a0, iclr_a2, gate_g2, k_k, k_a,
           r_k, lnx_g, lnx_b, w_out, ln1_g, ln1_b, peer_wq, peer_subkeys, peer_u, peer_v, ln2_g, ln2_b):
    B, S, D = x.shape
    depth = ada_w.shape[0]
    alpha = (2.0 * depth) ** 0.25
    heads = attn_sink.shape[1]
    rheads = r_k.shape[1]
    rw = rheads * HEAD_DIM
    aw = heads * HEAD_DIM
    kw = (w_in.shape[2] - aw - shift_mu.shape[1]) // 2
    kv_heads = kw // HEAD_DIM
    pheads, _, nkeys, half = peer_subkeys.shape[1:]
    assert S % WINDOW == 0 and S % CHUNK == 0 and (B * S) % PEER_TOK == 0
    T = B * S
    xt = x.reshape(T, D)
    for l in range(depth):
        mod3 = _ada(c, ada_w[l], ada_b[l]).reshape(B, 6, D)
        q, k, v, r, lw, kr, vr, kk, bb, g, bonus = _proj(
            xt, mod3, w_in[l].astype(BF16), S, aw, kw, shift_mu[l], decay_w0[l], decay_w2[l], iclr_a0[l],
            iclr_a2[l], gate_g2[l], k_k[l], k_a[l], r_k[l].reshape(-1))
        ya = _attn(attn_sink[l], q, k, v, B, S, heads, kv_heads)
        ys = _scan(r, lw, kr, vr, kk, bb, S, rheads)
        x1, h3 = _post(ya, ys, bonus, g, xt, mod3, lnx_g[l], lnx_b[l], w_out[l].astype(BF16), ln1_g[l], ln1_b[l],
                       S, alpha)
        rows, shifts, gates = _pkeys(h3, peer_wq[l].astype(BF16),
                                     peer_subkeys[l].reshape(pheads * 2, nkeys, half).astype(BF16),
                                     pheads, nkeys, half)
        coef = _peer_u(rows, shifts, h3, gates, _pack_table(peer_u[l]))
        y3 = _peer_v(rows, shifts, coef, _pack_table(peer_v[l]))
        xt = _ln2(x1, y3, mod3, ln2_g[l], ln2_b[l], S, alpha)
    return xt.reshape(B, S, D)
```
